```python
import math
import jax
import jax.numpy as jnp
from jax import lax
import numpy as np

D_MODEL = 1024
BATCH = 8
SEQ = 8192
DEPTH = 1
DEC_BATCH = 128
DEC_SEQ = 8
PAST_LEN = 8192
PAGE_SIZE = 128

DN_HEADS = 4
DN_DK = 128
DN_DV = 128
CONV_W = 4
DN_CHUNK = 64
MB_HEADS = 4
MB_DH = 128
MOBA_BLOCK = 256
MOBA_TOPK = 3
MOBA_QCHUNK = 64
ROPE_THETA = 10000.0
N_MEM = 256
X_HEADS = 4
X_DH = 128
N_EXPERTS = 32
TOP_K = 4
D_FF = D_MODEL
SWIGLU_ALPHA = 1.702
SWIGLU_LIMIT = 7.0
MOE_BLOCK = 256
EPS = 1e-6
NEG = -1e30

DN_QK = DN_HEADS * DN_DK
DN_V = DN_HEADS * DN_DV
MB_W = MB_HEADS * MB_DH
MIX_W = DN_V + MB_W
CONV_CH = 2 * DN_QK + DN_V
X_W = X_HEADS * X_DH
OFF_Z = CONV_CH
OFF_A = OFF_Z + DN_V
OFF_B = OFF_A + DN_HEADS
OFF_MB = OFF_B + DN_HEADS
IN_COLS = OFF_MB + 3 * MB_W

kernel_name = 'hybrid_deltanet_moba_moe_step'


def rmsnorm(x, g):
    xf = x.astype(jnp.float32)
    y = xf * lax.rsqrt(jnp.mean(xf * xf, axis=-1, keepdims=True) + EPS)
    return (y * g.astype(jnp.float32)).astype(x.dtype)


def l2norm(x):
    xf = x.astype(jnp.float32)
    return xf * lax.rsqrt(jnp.sum(xf * xf, axis=-1, keepdims=True) + EPS)


def rope(x, pos):
    half = x.shape[-1] // 2
    inv = ROPE_THETA ** (-jnp.arange(half, dtype=jnp.float32) / half)
    ang = pos.astype(jnp.float32)[:, None] * inv[None, :]
    cos = jnp.cos(ang)[None, :, None, :]
    sin = jnp.sin(ang)[None, :, None, :]
    xf = x.astype(jnp.float32)
    x1, x2 = xf[..., :half], xf[..., half:]
    return jnp.concatenate([x1 * cos - x2 * sin, x2 * cos + x1 * sin], axis=-1).astype(x.dtype)


def short_conv(u, buf, w):
    t = u.shape[1]
    full = jnp.concatenate([buf.astype(u.dtype), u], axis=1)
    y = full[:, 0:t] * w[0]
    for j in range(1, CONV_W):
        y = y + full[:, j:j + t] * w[j]
    return jax.nn.silu(y), full[:, t:]


def gated_delta_chunked(q, k, v, g, beta, s0):
    b, t, h, dk = q.shape
    c = DN_CHUNK
    n = -(-t // c)
    pad = n * c - t

    def chunks(a):
        a = jnp.pad(a, [(0, 0), (0, pad)] + [(0, 0)] * (a.ndim - 2))
        a = a.reshape((b, n, c) + a.shape[2:])
        return a.transpose((1, 0, 3, 2) + tuple(range(4, a.ndim)))

    qc = chunks(q * (dk ** -0.5))
    kc = chunks(k)
    vc = chunks(v)
    gc = chunks(g)
    bc = chunks(beta)
    gcum = jnp.cumsum(gc, axis=-1)
    ii = jnp.arange(c)
    causal = ii[:, None] >= ii[None, :]
    strict = ii[:, None] > ii[None, :]
    decay = jnp.exp(jnp.where(causal, gcum[..., :, None] - gcum[..., None, :], -jnp.inf))
    kbeta = kc * bc[..., None]
    a_mat = jnp.where(strict, jnp.einsum('nbhid,nbhjd->nbhij', kbeta, kc) * decay, 0.0)
    eye = jnp.eye(c, dtype=jnp.float32)
    t_inv = lax.linalg.triangular_solve(eye + a_mat, jnp.broadcast_to(eye, a_mat.shape),
                                        left_side=True, lower=True, unit_diagonal=True)
    u = t_inv @ (vc * bc[..., None])
    w = t_inv @ (kbeta * jnp.exp(gcum)[..., None])
    qk = jnp.einsum('nbhid,nbhjd->nbhij', qc, kc) * decay

    def step(s, xs):
        q_i, k_i, u_i, w_i, g_i, qk_i = xs
        v_new = u_i - w_i @ s
        o_i = (q_i * jnp.exp(g_i)[..., None]) @ s + qk_i @ v_new
        g_last = g_i[..., -1:]
        k_dec = k_i * jnp.exp(g_last - g_i)[..., None]
        s = s * jnp.exp(g_last)[..., None] + jnp.einsum('bhcd,bhce->bhde', k_dec, v_new)
        return s, o_i

    s_fin, o = lax.scan(step, s0, (qc, kc, u, w, gcum, qk))
    o = o.transpose(1, 0, 3, 2, 4).reshape(b, n * c, h, v.shape[-1])[:, :t]
    return o, s_fin


def deltanet_group(p, conv_buf, s0, conv_w, a_log, dt_bias, dn_norm):
    b, t, _ = p.shape
    qkv, conv_new = short_conv(p[..., :CONV_CH], conv_buf, conv_w)
    q = l2norm(qkv[..., :DN_QK].reshape(b, t, DN_HEADS, DN_DK))
    k = l2norm(qkv[..., DN_QK:2 * DN_QK].reshape(b, t, DN_HEADS, DN_DK))
    v = qkv[..., 2 * DN_QK:].reshape(b, t, DN_HEADS, DN_DV).astype(jnp.float32)
    z = p[..., OFF_Z:OFF_Z + DN_V].reshape(b, t, DN_HEADS, DN_DV).astype(jnp.float32)
    a = p[..., OFF_A:OFF_A + DN_HEADS].astype(jnp.float32)
    bb = p[..., OFF_B:OFF_B + DN_HEADS].astype(jnp.float32)
    g = -jnp.exp(a_log.astype(jnp.float32)) * jax.nn.softplus(a + dt_bias.astype(jnp.float32))
    beta = jax.nn.sigmoid(bb)
    o, s_new = gated_delta_chunked(q, k, v, g, beta, s0.astype(jnp.float32))
    o = rmsnorm(o, dn_norm) * jax.nn.silu(z)
    return o.reshape(b, t, DN_V).astype(p.dtype), s_new.astype(s0.dtype), conv_new


def moba_qkv(p, pos):
    b, t, _ = p.shape
    q = p[..., OFF_MB:OFF_MB + MB_W].reshape(b, t, MB_HEADS, MB_DH)
    k = p[..., OFF_MB + MB_W:OFF_MB + 2 * MB_W].reshape(b, t, MB_HEADS, MB_DH)
    v = p[..., OFF_MB + 2 * MB_W:OFF_MB + 3 * MB_W].reshape(b, t, MB_HEADS, MB_DH)
    return rope(q, pos), rope(k, pos), v


def moba_gate(q, kmean, b0, n_sel):
    nb = kmean.shape[2]
    s = jnp.einsum('bhtd,bhnd->bhtn', q.astype(jnp.float32), kmean)
    s = jnp.where(jnp.arange(nb) < b0, s, -jnp.inf)
    _, idx = lax.top_k(s, n_sel)
    return idx, idx < b0


def moba_attend(q, k_sel, v_sel, valid, k_own, v_own, own_pos, pos):
    scale = MB_DH ** -0.5
    s_own = jnp.einsum('bhtd,bhkd->bhtk', q, k_own).astype(jnp.float32) * scale
    s_own = jnp.where(own_pos[None, :] <= pos[:, None], s_own, NEG)
    if k_sel is None:
        p = jax.nn.softmax(s_own, axis=-1).astype(v_own.dtype)
        return jnp.einsum('bhtk,bhkd->bhtd', p, v_own)
    s_sel = jnp.einsum('bhtd,bhtkd->bhtk', q, k_sel).astype(jnp.float32) * scale
    s_sel = jnp.where(jnp.repeat(valid, MOBA_BLOCK, axis=-1), s_sel, NEG)
    p = jax.nn.softmax(jnp.concatenate([s_sel, s_own], axis=-1), axis=-1).astype(v_own.dtype)
    n_k = k_sel.shape[3]
    return (jnp.einsum('bhtk,bhtkd->bhtd', p[..., :n_k], v_sel)
            + jnp.einsum('bhtk,bhkd->bhtd', p[..., n_k:], v_own))


def moba_prompt(q, k, v):
    b, s, h, dh = q.shape
    nb = -(-s // MOBA_BLOCK)
    pad = nb * MOBA_BLOCK - s

    def blocks(a):
        a = jnp.pad(a, ((0, 0), (0, pad), (0, 0), (0, 0)))
        return a.reshape(b, nb, MOBA_BLOCK, h, dh).transpose(0, 3, 1, 2, 4)

    kb = blocks(k)
    vb = blocks(v)
    kmean = jnp.mean(kb.astype(jnp.float32), axis=3)
    qh = q.transpose(0, 2, 1, 3)
    bi = jnp.arange(b)[:, None, None, None]
    hi = jnp.arange(h)[None, :, None, None]
    n_sel = min(MOBA_TOPK, nb)

    def chunk_fn(ci):
        start = ci * MOBA_QCHUNK
        qc = lax.dynamic_slice_in_dim(qh, start, MOBA_QCHUNK, axis=2)
        pos = start + jnp.arange(MOBA_QCHUNK)
        b0 = start // MOBA_BLOCK
        idx, valid = moba_gate(qc, kmean, b0, n_sel)
        k_sel = kb[bi, hi, idx].reshape(b, h, MOBA_QCHUNK, n_sel * MOBA_BLOCK, dh)
        v_sel = vb[bi, hi, idx].reshape(b, h, MOBA_QCHUNK, n_sel * MOBA_BLOCK, dh)
        k_own = lax.dynamic_index_in_dim(kb, b0, axis=2, keepdims=False)
        v_own = lax.dynamic_index_in_dim(vb, b0, axis=2, keepdims=False)
        own_pos = b0 * MOBA_BLOCK + jnp.arange(MOBA_BLOCK)
        return moba_attend(qc, k_sel, v_sel, valid, k_own, v_own, own_pos, pos)

    o = lax.map(chunk_fn, jnp.arange(s // MOBA_QCHUNK))
    return o.transpose(1, 0, 3, 2, 4).reshape(b, s, h * dh)


def moba_sample(q, k_new, v_new, cache_k, cache_v, page_table, layer):
    bd, t, h, dh = q.shape
    page = cache_k.shape[2]
    n_pages = page_table.shape[1]
    past = n_pages * page
    ppb = MOBA_BLOCK // page
    n_full = past // MOBA_BLOCK
    full_pages = n_full * ppb
    r = past - n_full * MOBA_BLOCK
    qh = q.transpose(0, 2, 1, 3)
    pos = past + jnp.arange(t)
    tail = page_table[:, full_pages:]
    k_tail = cache_k[layer, tail].reshape(bd, r, h, dh)
    v_tail = cache_v[layer, tail].reshape(bd, r, h, dh)
    k_own = jnp.concatenate([k_tail, k_new.astype(k_tail.dtype)], axis=1).transpose(0, 2, 1, 3)
    v_own = jnp.concatenate([v_tail, v_new.astype(v_tail.dtype)], axis=1).transpose(0, 2, 1, 3)
    own_pos = n_full * MOBA_BLOCK + jnp.arange(r + t)
    if n_full == 0:
        o = moba_attend(qh, None, None, None, k_own, v_own, own_pos, pos)
        return o.transpose(0, 2, 1, 3).reshape(bd, t, h * dh)
    k_past = cache_k[layer, page_table[:, :full_pages]]
    kmean = jnp.mean(k_past.reshape(bd, n_full, MOBA_BLOCK, h, dh).astype(jnp.float32), axis=2).transpose(0, 2, 1, 3)
    n_sel = min(MOBA_TOPK, n_full)
    idx, valid = moba_gate(qh, kmean, n_full, n_sel)
    bi = jnp.arange(bd)[:, None, None, None]
    hi = jnp.arange(h)[None, :, None, None]
    sub = jnp.arange(ppb)

    def token_fn(args):
        q_t, idx_t, valid_t, pos_t = args
        phys = page_table[bi, idx_t[..., None] * ppb + sub]
        k_sel = cache_k[layer, phys, :, hi].reshape(bd, h, 1, n_sel * MOBA_BLOCK, dh)
        v_sel = cache_v[layer, phys, :, hi].reshape(bd, h, 1, n_sel * MOBA_BLOCK, dh)
        o_t = moba_attend(q_t[:, :, None].astype(k_sel.dtype), k_sel, v_sel, valid_t[:, :, None],
                          k_own, v_own, own_pos, pos_t[None])
        return o_t[:, :, 0]

    o = lax.map(token_fn, (qh.transpose(2, 0, 1, 3), idx.transpose(2, 0, 1, 3),
                           valid.transpose(2, 0, 1, 3), pos))
    return o.transpose(1, 0, 2, 3).reshape(bd, t, h * dh).astype(q.dtype)


def cross_attn(hx, mk, mv, w_q, w_o):
    b, t, _ = hx.shape
    q = (hx @ w_q).reshape(b, t, X_HEADS, X_DH)
    s = jnp.einsum('bthd,bmhd->bhtm', q, mk.astype(q.dtype)).astype(jnp.float32) * (X_DH ** -0.5)
    p = jax.nn.softmax(s, axis=-1).astype(q.dtype)
    o = jnp.einsum('bhtm,bmhd->bthd', p, mv.astype(q.dtype)).reshape(b, t, X_W)
    return o @ w_o


def moe(hx, w_router, b_router, w_gate, b_gate, w_up, b_up, w_down, b_down):
    shp = hx.shape
    x = hx.reshape(-1, shp[-1])
    n = x.shape[0]
    logits = (x @ w_router).astype(jnp.float32) + b_router.astype(jnp.float32)
    top_logit, top_e = lax.top_k(logits, TOP_K)
    gates = jax.nn.softmax(top_logit, axis=-1)
    flat_e = top_e.reshape(-1)
    nk = n * TOP_K
    order = jnp.argsort(flat_e)
    e_sorted = flat_e[order]
    counts = jnp.bincount(flat_e, length=N_EXPERTS)
    padded = (counts + MOE_BLOCK - 1) // MOE_BLOCK * MOE_BLOCK
    pad_end = jnp.cumsum(padded)
    pad_start = pad_end - padded
    start = jnp.cumsum(counts) - counts
    dest = pad_start[e_sorted] + jnp.arange(nk) - start[e_sorted]
    n_blocks = -(-(nk + N_EXPERTS * (MOE_BLOCK - 1)) // MOE_BLOCK)
    n_slots = n_blocks * MOE_BLOCK
    tok_sorted = order // TOP_K
    slot_token = jnp.full((n_slots,), n, jnp.int32).at[dest].set(tok_sorted.astype(jnp.int32))
    x_pad = jnp.concatenate([x, jnp.zeros((1, x.shape[1]), x.dtype)], axis=0)
    xs = x_pad[slot_token].reshape(n_blocks, MOE_BLOCK, x.shape[1])
    block_e = jnp.minimum(jnp.searchsorted(pad_end, jnp.arange(n_blocks) * MOE_BLOCK, side='right'), N_EXPERTS - 1)

    def expert_block(args):
        xb, e = args
        g = xb @ w_gate[e] + b_gate[e]
        u = xb @ w_up[e] + b_up[e]
        g = jnp.minimum(g, SWIGLU_LIMIT)
        u = jnp.clip(u, -SWIGLU_LIMIT, SWIGLU_LIMIT)
        a = g * jax.nn.sigmoid(SWIGLU_ALPHA * g) * (u + 1.0)
        return a @ w_down[e] + b_down[e]

    ys = lax.map(expert_block, (xs, block_e)).reshape(n_slots, x.shape[1])
    gate_sorted = gates.reshape(-1)[order].astype(x.dtype)
    out = jax.ops.segment_sum(ys[dest] * gate_sorted[:, None], tok_sorted, num_segments=n)
    return out.reshape(shp)


def setup_inputs(seed: int = 0) -> dict:
    key = jax.random.key(seed)
    keys = iter(jax.random.split(key, 48))
    f32 = jnp.float32
    n_pages = PAST_LEN // PAGE_SIZE
    n_pool = (DEC_BATCH * n_pages * 5) // 4

    def nrm(shape, scale):
        return jax.random.normal(next(keys), shape, f32) * scale

    def gain(shape):
        return 1.0 + 0.02 * jax.random.normal(next(keys), shape, f32)

    perm = jax.random.permutation(next(keys), n_pool)
    page_table = perm[:DEC_BATCH * n_pages].reshape(DEC_BATCH, n_pages).astype(jnp.int32)
    a_log = jnp.log(jax.random.uniform(next(keys), (DEPTH, DN_HEADS), f32, 1.0, 16.0))
    dt = jnp.exp(jax.random.uniform(next(keys), (DEPTH, DN_HEADS), f32, math.log(1e-3), math.log(1e-1)))
    dt_bias = dt + jnp.log(-jnp.expm1(-dt))
    return {
        'x_prompt': nrm((BATCH, SEQ, D_MODEL), 1.0),
        'x_sample': nrm((DEC_BATCH, DEC_SEQ, D_MODEL), 1.0),
        'mem_prompt': nrm((BATCH, N_MEM, D_MODEL), 1.0),
        'cache_k': nrm((DEPTH, n_pool, PAGE_SIZE, MB_HEADS, MB_DH), 1.0),
        'cache_v': nrm((DEPTH, n_pool, PAGE_SIZE, MB_HEADS, MB_DH), 1.0),
        'page_table': page_table,
        'state_delta': nrm((DEPTH, DEC_BATCH, DN_HEADS, DN_DK, DN_DV), 0.1),
        'state_conv': nrm((DEPTH, DEC_BATCH, CONV_W - 1, CONV_CH), 1.0),
        'cache_mem_k': nrm((DEPTH, DEC_BATCH, N_MEM, X_HEADS, X_DH), 1.0),
        'cache_mem_v': nrm((DEPTH, DEC_BATCH, N_MEM, X_HEADS, X_DH), 1.0),
        'norm_mix': gain((DEPTH, D_MODEL)),
        'w_in': nrm((DEPTH, D_MODEL, IN_COLS), D_MODEL ** -0.5),
        'conv_w': nrm((DEPTH, CONV_W, CONV_CH), CONV_W ** -0.5),
        'a_log': a_log,
        'dt_bias': dt_bias,
        'dn_norm': gain((DEPTH, DN_DV)),
        'w_out': nrm((DEPTH, MIX_W, D_MODEL), MIX_W ** -0.5),
        'norm_x': gain((DEPTH, D_MODEL)),
        'norm_mem': gain((DEPTH, D_MODEL)),
        'w_xq': nrm((DEPTH, D_MODEL, X_W), D_MODEL ** -0.5),
        'w_xk': nrm((DEPTH, D_MODEL, X_W), D_MODEL ** -0.5),
        'w_xv': nrm((DEPTH, D_MODEL, X_W), D_MODEL ** -0.5),
        'w_xo': nrm((DEPTH, X_W, D_MODEL), X_W ** -0.5),
        'norm_ffn': gain((DEPTH, D_MODEL)),
        'w_router': nrm((DEPTH, D_MODEL, N_EXPERTS), D_MODEL ** -0.5),
        'b_router': nrm((DEPTH, N_EXPERTS), 0.01),
        'w_gate': nrm((DEPTH, N_EXPERTS, D_MODEL, D_FF), D_MODEL ** -0.5),
        'b_gate': nrm((DEPTH, N_EXPERTS, D_FF), 0.01),
        'w_up': nrm((DEPTH, N_EXPERTS, D_MODEL, D_FF), D_MODEL ** -0.5),
        'b_up': nrm((DEPTH, N_EXPERTS, D_FF), 0.01),
        'w_down': nrm((DEPTH, N_EXPERTS, D_FF, D_MODEL), D_FF ** -0.5),
        'b_down': nrm((DEPTH, N_EXPERTS, D_MODEL), 0.01),
        'norm_final': gain((D_MODEL,)),
    }


def reference(x_prompt, x_sample, mem_prompt, cache_k, cache_v, page_table, state_delta, state_conv,
              cache_mem_k, cache_mem_v, norm_mix, w_in, conv_w, a_log, dt_bias, dn_norm, w_out,
              norm_x, norm_mem, w_xq, w_xk, w_xv, w_xo, norm_ffn, w_router, b_router,
              w_gate, b_gate, w_up, b_up, w_down, b_down, norm_final):
    b, s, _ = x_prompt.shape
    bd, t, _ = x_sample.shape
    n_mem = mem_prompt.shape[1]
    past = page_table.shape[1] * cache_k.shape[2]
    pos_p = jnp.arange(s)
    pos_s = past + jnp.arange(t)
    xp, xs = x_prompt, x_sample
    kp_l, vp_l, ks_l, vs_l, dp_l, ds_l, cp_l, cs_l, mkp_l, mvp_l = [], [], [], [], [], [], [], [], [], []
    for l in range(DEPTH):
        pp = rmsnorm(xp, norm_mix[l]) @ w_in[l]
        dn_p, s_p, cv_p = deltanet_group(pp, jnp.zeros((b, CONV_W - 1, CONV_CH), pp.dtype),
                                         jnp.zeros((b, DN_HEADS, DN_DK, DN_DV), jnp.float32),
                                         conv_w[l], a_log[l], dt_bias[l], dn_norm[l])
        qp, kp, vp = moba_qkv(pp, pos_p)
        mb_p = moba_prompt(qp, kp, vp)
        xp = xp + jnp.concatenate([dn_p, mb_p], axis=-1) @ w_out[l]
        mn = rmsnorm(mem_prompt, norm_mem[l])
        mk_p = (mn @ w_xk[l]).reshape(b, n_mem, X_HEADS, X_DH)
        mv_p = (mn @ w_xv[l]).reshape(b, n_mem, X_HEADS, X_DH)
        xp = xp + cross_attn(rmsnorm(xp, norm_x[l]), mk_p, mv_p, w_xq[l], w_xo[l])
        xp = xp + moe(rmsnorm(xp, norm_ffn[l]), w_router[l], b_router[l], w_gate[l], b_gate[l],
                      w_up[l], b_up[l], w_down[l], b_down[l])
        ps = rmsnorm(xs, norm_mix[l]) @ w_in[l]
        dn_s, s_s, cv_s = deltanet_group(ps, state_conv[l], state_delta[l],
                                         conv_w[l], a_log[l], dt_bias[l], dn_norm[l])
        qs, ks, vs = moba_qkv(ps, pos_s)
        mb_s = moba_sample(qs, ks, vs, cache_k, cache_v, page_table, l)
        xs = xs + jnp.concatenate([dn_s, mb_s.astype(dn_s.dtype)], axis=-1) @ w_out[l]
        xs = xs + cross_attn(rmsnorm(xs, norm_x[l]), cache_mem_k[l], cache_mem_v[l], w_xq[l], w_xo[l])
        xs = xs + moe(rmsnorm(xs, norm_ffn[l]), w_router[l], b_router[l], w_gate[l], b_gate[l],
                      w_up[l], b_up[l], w_down[l], b_down[l])
        kp_l.append(kp)
        vp_l.append(vp)
        ks_l.append(ks)
        vs_l.append(vs)
        dp_l.append(s_p)
        ds_l.append(s_s)
        cp_l.append(cv_p)
        cs_l.append(cv_s)
        mkp_l.append(mk_p)
        mvp_l.append(mv_p)
    y_prompt = rmsnorm(xp, norm_final)
    y_sample = rmsnorm(xs, norm_final)
    return (y_prompt, y_sample, jnp.stack(kp_l), jnp.stack(vp_l), jnp.stack(ks_l), jnp.stack(vs_l),
            jnp.stack(dp_l), jnp.stack(ds_l), jnp.stack(cp_l), jnp.stack(cs_l),
            jnp.stack(mkp_l), jnp.stack(mvp_l))
```

```python
import functools

import jax
import jax.numpy as jnp
from jax import lax
from jax.experimental import pallas as pl
from jax.experimental.pallas import tpu as pltpu

F32 = jnp.float32
BF16 = jnp.bfloat16
I32 = jnp.int32

HEADS = 4
DH = 128
HW = HEADS * DH
CONV_W = 4
DN_CHUNK = 64
MOBA_BLOCK = 256
MOBA_TOPK = 3
ROPE_THETA = 10000.0
TOP_K = 4
SWIGLU_ALPHA = 1.702
SWIGLU_LIMIT = 7.0
EPS = 1e-6
NEG_INF = float("-inf")

LANES = 128
SUBLANES = 8
VMEM_LIMIT = 56 * 1024 * 1024

ROW_TILE = 512
DN_GROUP = 256
MOE_BLOCK_ROWS = 512
MOVE_TILE = 256
PAGES_PER_STEP = 16
SAMPLE_SEQS = 8

NN_DIMS = (((1,), (0,)), ((), ()))
NT_DIMS = (((1,), (1,)), ((), ()))
TN_DIMS = (((0,), (0,)), ((), ()))


def _mm(a, b, dims=NN_DIMS):
    return lax.dot_general(a.astype(BF16), b.astype(BF16), dims, preferred_element_type=F32)


def _split(a):
    hi = a.astype(BF16)
    lo = (a - hi.astype(F32)).astype(BF16)
    return hi, lo


def _mm3(a, b, dims=NN_DIMS):
    a_hi, a_lo = _split(a)
    b_hi, b_lo = _split(b)
    d = functools.partial(lax.dot_general, dimension_numbers=dims, preferred_element_type=F32)
    return d(a_hi, b_hi) + (d(a_hi, b_lo) + d(a_lo, b_hi))


def _rms(x, w):
    return x * lax.rsqrt(jnp.mean(x * x, axis=-1, keepdims=True) + EPS) * w


def _sigmoid(x):
    return 1.0 / (1.0 + jnp.exp(-x))


def _softplus(x):
    return jnp.maximum(x, 0.0) + jnp.log1p(jnp.exp(-jnp.abs(x)))


def _params(*sem):
    return pltpu.CompilerParams(dimension_semantics=sem, vmem_limit_bytes=VMEM_LIMIT)


def _full(shape):
    nd = len(shape)
    return pl.BlockSpec(shape, lambda *_: (0,) * nd)


def _top_picks(scores, n_pick, ids, n_ids, axis):
    picks = []
    cur = scores
    for _ in range(n_pick):
        m = jnp.max(cur, axis=axis, keepdims=True)
        idx = jnp.min(jnp.where(cur == m, ids, n_ids), axis=axis, keepdims=True)
        picks.append((idx, m))
        cur = jnp.where(ids == idx, NEG_INF, cur)
    return picks


def _memkv_kernel(m_ref, nw_ref, wk_ref, wv_ref, k_ref, v_ref):
    h = _rms(m_ref[...], nw_ref[...])
    k_ref[...] = _mm(h, wk_ref[...])
    v_ref[...] = _mm(h, wv_ref[...])


def _mem_kv(mem, norm_mem, w_xk, w_xv):
    n, d = mem.shape
    tm = min(ROW_TILE, n)
    return pl.pallas_call(
        _memkv_kernel,
        grid=(n // tm,),
        in_specs=[pl.BlockSpec((tm, d), lambda i: (i, 0)), _full((1, d)), _full(w_xk.shape), _full(w_xv.shape)],
        out_specs=[pl.BlockSpec((tm, HW), lambda i: (i, 0))] * 2,
        out_shape=[jax.ShapeDtypeStruct((n, HW), F32)] * 2,
        compiler_params=_params("parallel"),
        name="mem_kv",
    )(mem, norm_mem.reshape(1, d), w_xk.astype(BF16), w_xv.astype(BF16))


def _rope(x, cos, sin):
    return x * cos + pltpu.roll(x, DH // 2, axis=1) * sin


def _inproj_kernel(x_ref, nw_ref, w_ref, wab_ref, cos_ref, sin_ref, alog_ref, dtb_ref,
                   conv_ref, z_ref, gb_ref, q_ref, k_ref, v_ref, *rest, conv_ch, prompt):
    x = x_ref[...]
    hb = _rms(x, nw_ref[...]).astype(BF16)
    conv_ref[...] = _mm(hb, w_ref[:, 0:conv_ch])
    z_ref[...] = _mm(hb, w_ref[:, conv_ch:conv_ch + HW])
    ab = _mm(hb, wab_ref[...])
    lane = lax.broadcasted_iota(I32, ab.shape, 1)
    g = -jnp.exp(alog_ref[...]) * _softplus(ab + dtb_ref[...])
    gb_ref[...] = jnp.where(lane < HEADS, g, _sigmoid(ab))
    off = conv_ch + HW
    cos = cos_ref[...]
    sin = sin_ref[...]
    q = _mm(hb, w_ref[:, off:off + HW])
    k = _mm(hb, w_ref[:, off + HW:off + 2 * HW])
    v = _mm(hb, w_ref[:, off + 2 * HW:off + 3 * HW])
    k_rot = []
    for h in range(HEADS):
        sl = slice(h * DH, (h + 1) * DH)
        q_ref[:, sl] = _rope(q[:, sl], cos, sin)
        k_rot.append(_rope(k[:, sl], cos, sin))
        k_ref[:, sl] = k_rot[h]
    v_ref[...] = v
    if prompt:
        kb_ref, vt_ref, kmean_ref = rest
        for h in range(HEADS):
            kb_ref[:, h * DH:(h + 1) * DH] = k_rot[h].astype(BF16)
        for blk in range(x.shape[0] // MOBA_BLOCK):
            rows = slice(blk * MOBA_BLOCK, (blk + 1) * MOBA_BLOCK)
            vt_ref[blk] = v[rows].T.astype(BF16)
            for h in range(HEADS):
                kmean_ref[blk, :, h * DH:(h + 1) * DH] = jnp.mean(k_rot[h][rows], axis=0, keepdims=True)


def _in_proj(x, norm_w, w_in, a_log, dt_bias, cos, sin, *, prompt):
    n, d = x.shape
    conv_ch = 3 * HW
    tm = min(ROW_TILE, n)
    off_a = conv_ch + HW
    w_main = jnp.concatenate([w_in[:, :off_a], w_in[:, off_a + 2 * HEADS:]], axis=1).astype(BF16)
    w_ab = jnp.pad(w_in[:, off_a:off_a + 2 * HEADS], ((0, 0), (0, LANES - 2 * HEADS))).astype(BF16)
    alog = jnp.pad(a_log, (0, LANES - HEADS)).reshape(1, LANES)
    dtb = jnp.pad(dt_bias, (0, LANES - HEADS)).reshape(1, LANES)
    n_pos = cos.shape[0] // tm
    row = lambda w: pl.BlockSpec((tm, w), lambda i: (i, 0))
    out_specs = [row(conv_ch), row(HW), row(LANES), row(HW), row(HW), row(HW)]
    out_shape = [jax.ShapeDtypeStruct((n, w), F32) for w in (conv_ch, HW, LANES, HW, HW, HW)]
    if prompt:
        nb = tm // MOBA_BLOCK
        out_specs += [row(HW),
                      pl.BlockSpec((nb, HW, MOBA_BLOCK), lambda i: (i, 0, 0)),
                      pl.BlockSpec((nb, 1, HW), lambda i: (i, 0, 0))]
        out_shape += [jax.ShapeDtypeStruct((n, HW), BF16),
                      jax.ShapeDtypeStruct((n // MOBA_BLOCK, HW, MOBA_BLOCK), BF16),
                      jax.ShapeDtypeStruct((n // MOBA_BLOCK, 1, HW), F32)]
    return pl.pallas_call(
        functools.partial(_inproj_kernel, conv_ch=conv_ch, prompt=prompt),
        grid=(n // tm,),
        in_specs=[row(d), _full((1, d)), _full(w_main.shape), _full(w_ab.shape),
                  pl.BlockSpec((tm, DH), lambda i: (i % n_pos, 0)),
                  pl.BlockSpec((tm, DH), lambda i: (i % n_pos, 0)),
                  _full((1, LANES)), _full((1, LANES))],
        out_specs=out_specs,
        out_shape=out_shape,
        compiler_params=_params("parallel"),
        name="in_proj",
    )(x, norm_w.reshape(1, d), w_main, w_ab, cos, sin, alog, dtb)


def _rope_tables(pos):
    half = DH // 2
    inv = ROPE_THETA ** (-jnp.arange(half, dtype=F32) / half)
    ang = pos.astype(F32)[:, None] * inv[None, :]
    c, s = jnp.cos(ang), jnp.sin(ang)
    return jnp.concatenate([c, c], axis=1), jnp.concatenate([-s, s], axis=1)


def _chunk_cumsum(g, row_in_chunk):
    for sh in (1, 2, 4, 8, 16, 32):
        g = g + jnp.where(row_in_chunk >= sh, pltpu.roll(g, sh, axis=0), 0.0)
    return g


def _unit_lower_inverse(a):
    n = a.shape[0]
    eye = (lax.broadcasted_iota(I32, (n, n), 0) == lax.broadcasted_iota(I32, (n, n), 1)).astype(F32)
    p = eye - a
    ak = a
    for _ in range(5):
        ak = _mm3(ak, ak)
        p = p + _mm3(p, ak)
    return p


def _gated_out(o, z, dn_w):
    return _rms(o, dn_w) * (z * _sigmoid(z))


def _dn_prompt_kernel(u_ref, halo_ref, z_ref, gb_ref, cw_ref, dnw_ref,
                      o_ref, sfin_ref, cnew_ref,
                      ext_ref, s_ref, wq_ref, us_ref, kd_ref, vn_ref, oi_ref, qk_ref, gl_ref, *, tm):
    i = pl.program_id(1)
    nt = pl.num_programs(1)
    nc = tm // DN_CHUNK
    cpg = DN_GROUP // DN_CHUNK
    hal = CONV_W - 1

    @pl.when(i == 0)
    def _():
        s_ref[...] = jnp.zeros_like(s_ref)

    u = u_ref[...]
    ext_ref[0:SUBLANES, :] = jnp.where(i > 0, halo_ref[...], 0.0)
    ext_ref[SUBLANES:SUBLANES + tm, :] = u
    cw = cw_ref[...]
    y = u * cw[hal:hal + 1]
    for j in range(hal):
        y = y + ext_ref[SUBLANES - hal + j:SUBLANES - hal + j + tm, :] * cw[j:j + 1]
    qkv = y * _sigmoid(y)

    gb = gb_ref[...]
    row_in_chunk = lax.broadcasted_iota(I32, gb.shape, 0) % DN_CHUNK
    gc = _chunk_cumsum(gb, row_in_chunk)
    gc3 = gc.reshape(nc, DN_CHUNK, LANES)
    glast = jnp.broadcast_to(gc3[:, DN_CHUNK - 1:DN_CHUNK, :], gc3.shape).reshape(tm, LANES)
    gl_ref[...] = jnp.exp(glast)
    gct = gc.T

    gi = lax.broadcasted_iota(I32, (DN_GROUP, DN_GROUP), 0)
    gj = lax.broadcasted_iota(I32, (DN_GROUP, DN_GROUP), 1)
    same = (gi // DN_CHUNK) == (gj // DN_CHUNK)
    causal = same & (gi >= gj)
    strict = same & (gi > gj)

    for h in range(HEADS):
        qh = qkv[:, h * DH:(h + 1) * DH]
        kh = qkv[:, HW + h * DH:HW + (h + 1) * DH]
        vh = qkv[:, 2 * HW + h * DH:2 * HW + (h + 1) * DH]
        qn = qh * lax.rsqrt(jnp.sum(qh * qh, axis=-1, keepdims=True) + EPS) * (DH ** -0.5)
        kn = kh * lax.rsqrt(jnp.sum(kh * kh, axis=-1, keepdims=True) + EPS)
        gcol = gc[:, h:h + 1]
        beta = gb[:, HEADS + h:HEADS + h + 1]
        eg = jnp.exp(gcol)
        kbeta = kn * beta
        kd_ref[h] = kn * jnp.exp(glast[:, h:h + 1] - gcol)
        qe = qn * eg
        for r in range(tm // DN_GROUP):
            rs = slice(r * DN_GROUP, (r + 1) * DN_GROUP)
            decay = jnp.where(causal, jnp.exp(jnp.minimum(gcol[rs] - gct[h:h + 1, rs], 0.0)), 0.0)
            a = jnp.where(strict, _mm3(kbeta[rs], kn[rs], NT_DIMS) * decay, 0.0)
            qk_ref[h, rs, :] = _mm3(qn[rs], kn[rs], NT_DIMS) * decay
            t_inv = _unit_lower_inverse(a)
            rhs = jnp.concatenate([vh[rs] * beta[rs], kbeta[rs] * eg[rs]], axis=1)
            uw = _mm3(t_inv, rhs)
            us_ref[h, rs, :] = uw[:, :DH]
            for c in range(cpg):
                lo = r * DN_GROUP + c * DN_CHUNK
                wq_ref[h, r * cpg + c, 0:DN_CHUNK, :] = uw[c * DN_CHUNK:(c + 1) * DN_CHUNK, DH:]
                wq_ref[h, r * cpg + c, DN_CHUNK:2 * DN_CHUNK, :] = qe[lo:lo + DN_CHUNK]

    def chunk_step(c, carry):
        r0 = pl.multiple_of(c * DN_CHUNK, DN_CHUNK)
        rows = pl.ds(r0, DN_CHUNK)
        for h in range(HEADS):
            s = s_ref[h]
            m1 = _mm3(wq_ref[h, c], s)
            v_new = us_ref[h, rows, :] - m1[:DN_CHUNK]
            oi_ref[h, rows, :] = m1[DN_CHUNK:]
            vn_ref[h, rows, :] = v_new
            s_ref[h] = s * gl_ref[pl.ds(r0, 1), h:h + 1] + _mm3(kd_ref[h, rows, :], v_new, TN_DIMS)
        return carry

    lax.fori_loop(0, nc, chunk_step, 0)

    z = z_ref[...]
    dnw = dnw_ref[...]
    for h in range(HEADS):
        parts = []
        for r in range(tm // DN_GROUP):
            rs = slice(r * DN_GROUP, (r + 1) * DN_GROUP)
            parts.append(oi_ref[h, rs, :] + _mm3(qk_ref[h, rs, :], vn_ref[h, rs, :]))
        o = jnp.concatenate(parts, axis=0)
        o_ref[:, h * DH:(h + 1) * DH] = _gated_out(o, z[:, h * DH:(h + 1) * DH], dnw).astype(o_ref.dtype)

    @pl.when(i == nt - 1)
    def _():
        sfin_ref[0] = s_ref[...]
        cnew_ref[0] = ext_ref[SUBLANES + tm - hal:SUBLANES + tm, :]


def _deltanet_prompt(conv_in, z, gb, conv_w, dn_norm, batch):
    n, conv_ch = conv_in.shape
    s = n // batch
    tm = min(ROW_TILE, s)
    nt = s // tm
    hal = CONV_W - 1
    hb = tm // SUBLANES
    row = lambda w: pl.BlockSpec((tm, w), lambda b, i: (b * nt + i, 0))
    head_scr = lambda w: pltpu.VMEM((HEADS, tm, w), F32)
    return pl.pallas_call(
        functools.partial(_dn_prompt_kernel, tm=tm),
        grid=(batch, nt),
        in_specs=[row(conv_ch),
                  pl.BlockSpec((SUBLANES, conv_ch), lambda b, i: (jnp.maximum((b * nt + i) * hb - 1, 0), 0)),
                  row(HW), row(LANES), _full((CONV_W, conv_ch)), _full((1, DH))],
        out_specs=[row(HW),
                   pl.BlockSpec((1, HEADS, DH, DH), lambda b, i: (b, 0, 0, 0)),
                   pl.BlockSpec((1, hal, conv_ch), lambda b, i: (b, 0, 0))],
        out_shape=[jax.ShapeDtypeStruct((n, HW), BF16),
                   jax.ShapeDtypeStruct((batch, HEADS, DH, DH), F32),
                   jax.ShapeDtypeStruct((batch, hal, conv_ch), F32)],
        scratch_shapes=[pltpu.VMEM((SUBLANES + tm, conv_ch), F32),
                        pltpu.VMEM((HEADS, DH, DH), F32),
                        pltpu.VMEM((HEADS, tm // DN_CHUNK, 2 * DN_CHUNK, DH), F32),
                        head_scr(DH), head_scr(DH), head_scr(DH), head_scr(DH), head_scr(DN_GROUP),
                        pltpu.VMEM((tm, LANES), F32)],
        compiler_params=_params("arbitrary", "arbitrary"),
        name="deltanet_prompt",
    )(conv_in, conv_in, z, gb, conv_w, dn_norm.reshape(1, DH))


def _dn_sample_kernel(u_ref, cbuf_ref, z_ref, gb_ref, st_ref, cw_ref, dnw_ref,
                      o_ref, snew_ref, cnew_ref, ext_ref, *, t, seqs):
    hal = CONV_W - 1
    cw = cw_ref[...]
    dnw = dnw_ref[...]

    def seq_step(si, carry):
        r0 = pl.multiple_of(si * t, t)
        rows = pl.ds(r0, t)
        ext_ref[SUBLANES - hal:SUBLANES, :] = cbuf_ref[si]
        ext_ref[SUBLANES:SUBLANES + t, :] = u_ref[rows, :]
        y = ext_ref[SUBLANES - hal:SUBLANES - hal + t, :] * cw[0:1]
        for j in range(1, CONV_W):
            y = y + ext_ref[SUBLANES - hal + j:SUBLANES - hal + j + t, :] * cw[j:j + 1]
        cnew_ref[si] = ext_ref[SUBLANES + t - hal:SUBLANES + t, :]
        qkv = y * _sigmoid(y)
        gb = gb_ref[rows, :]
        z = z_ref[rows, :]
        for h in range(HEADS):
            qh = qkv[:, h * DH:(h + 1) * DH]
            kh = qkv[:, HW + h * DH:HW + (h + 1) * DH]
            vh = qkv[:, 2 * HW + h * DH:2 * HW + (h + 1) * DH]
            qn = qh * lax.rsqrt(jnp.sum(qh * qh, axis=-1, keepdims=True) + EPS) * (DH ** -0.5)
            kn = kh * lax.rsqrt(jnp.sum(kh * kh, axis=-1, keepdims=True) + EPS)
            qt = qn.T
            kt = kn.T
            eg = jnp.exp(gb[:, h:h + 1])
            beta = gb[:, HEADS + h:HEADS + h + 1]
            s = st_ref[si, h]
            outs = []
            for ti in range(t):
                kcol = kt[:, ti:ti + 1]
                s = s * eg[ti:ti + 1]
                v_new = beta[ti:ti + 1] * (vh[ti:ti + 1] - jnp.sum(s * kcol, axis=0, keepdims=True))
                s = s + kcol * v_new
                outs.append(jnp.sum(s * qt[:, ti:ti + 1], axis=0, keepdims=True))
            snew_ref[si, h] = s
            o = jnp.concatenate(outs, axis=0)
            o_ref[rows, h * DH:(h + 1) * DH] = _gated_out(o, z[:, h * DH:(h + 1) * DH], dnw)
        return carry

    lax.fori_loop(0, seqs, seq_step, 0)


def _deltanet_sample(conv_in, z, gb, state_conv, state_delta, conv_w, dn_norm):
    n, conv_ch = conv_in.shape
    bd = state_delta.shape[0]
    t = n // bd
    seqs = min(SAMPLE_SEQS, bd)
    hal = CONV_W - 1
    row = lambda w: pl.BlockSpec((seqs * t, w), lambda i: (i, 0))
    st_spec = pl.BlockSpec((seqs, HEADS, DH, DH), lambda i: (i, 0, 0, 0))
    cb_spec = pl.BlockSpec((seqs, hal, conv_ch), lambda i: (i, 0, 0))
    return pl.pallas_call(
        functools.partial(_dn_sample_kernel, t=t, seqs=seqs),
        grid=(bd // seqs,),
        in_specs=[row(conv_ch), cb_spec, row(HW), row(LANES), st_spec, _full((CONV_W, conv_ch)), _full((1, DH))],
        out_specs=[row(HW), st_spec, cb_spec],
        out_shape=[jax.ShapeDtypeStruct((n, HW), F32),
                   jax.ShapeDtypeStruct(state_delta.shape, F32),
                   jax.ShapeDtypeStruct(state_conv.shape, F32)],
        scratch_shapes=[pltpu.VMEM((SUBLANES + t, conv_ch), F32)],
        compiler_params=_params("parallel"),
        name="deltanet_sample",
    )(conv_in, state_conv, z, gb, state_delta, conv_w, dn_norm.reshape(1, DH))


def _moba_prompt_kernel(q_ref, k_ref, vt_ref, km_ref, o_ref, sel_ref, m_ref, l_ref, acc_ref, *, nb):
    qb = pl.program_id(2)
    blk = MOBA_BLOCK
    scale = DH ** -0.5
    q = q_ref[...]
    qbf = q.astype(BF16)

    gs = _mm3(km_ref[0], q, NT_DIMS)
    brow = lax.broadcasted_iota(I32, gs.shape, 0)
    gs = jnp.where(brow < qb, gs, NEG_INF)
    sel = jnp.zeros(gs.shape, F32)
    for idx, val in _top_picks(gs, MOBA_TOPK, brow, nb, 0):
        sel = jnp.where((brow == idx) & (val > NEG_INF), 1.0, sel)
    sel_ref[...] = sel

    def scores(j):
        kj = k_ref[pl.ds(pl.multiple_of(j * blk, blk), blk), :]
        return lax.dot_general(kj, qbf, NT_DIMS, preferred_element_type=F32) * scale

    ki = lax.broadcasted_iota(I32, (blk, blk), 0)
    qi = lax.broadcasted_iota(I32, (blk, blk), 1)
    s = jnp.where(ki <= qi, scores(qb), NEG_INF)
    m = jnp.max(s, axis=0, keepdims=True)
    p = jnp.exp(s - m)
    m_ref[...] = m
    l_ref[...] = jnp.sum(p, axis=0, keepdims=True)
    acc_ref[...] = _mm(vt_ref[qb], p)

    def past_block(j, carry):
        s = jnp.where(sel_ref[pl.ds(j, 1), :] > 0.5, scores(j), NEG_INF)
        m_old = m_ref[...]
        m_new = jnp.maximum(m_old, jnp.max(s, axis=0, keepdims=True))
        alpha = jnp.exp(m_old - m_new)
        p = jnp.exp(s - m_new)
        m_ref[...] = m_new
        l_ref[...] = alpha * l_ref[...] + jnp.sum(p, axis=0, keepdims=True)
        acc_ref[...] = alpha * acc_ref[...] + _mm(vt_ref[j], p)
        return carry

    lax.fori_loop(0, qb, past_block, 0)
    o_ref[...] = (acc_ref[...] / l_ref[...]).T.astype(o_ref.dtype)


def _moba_prompt(q, kb, vt, kmean, batch):
    n = q.shape[0]
    s = n // batch
    nb = s // MOBA_BLOCK
    return pl.pallas_call(
        functools.partial(_moba_prompt_kernel, nb=nb),
        grid=(batch, HEADS, nb),
        in_specs=[pl.BlockSpec((MOBA_BLOCK, DH), lambda b, h, j: (b * nb + j, h)),
                  pl.BlockSpec((s, DH), lambda b, h, j: (b, h)),
                  pl.BlockSpec((nb, DH, MOBA_BLOCK), lambda b, h, j: (b, h, 0)),
                  pl.BlockSpec((1, nb, DH), lambda b, h, j: (b, 0, h))],
        out_specs=pl.BlockSpec((MOBA_BLOCK, DH), lambda b, h, j: (b * nb + j, h)),
        out_shape=jax.ShapeDtypeStruct((n, HW), BF16),
        scratch_shapes=[pltpu.VMEM((nb, MOBA_BLOCK), F32), pltpu.VMEM((1, MOBA_BLOCK), F32),
                        pltpu.VMEM((1, MOBA_BLOCK), F32), pltpu.VMEM((DH, MOBA_BLOCK), F32)],
        compiler_params=_params("parallel", "parallel", "arbitrary"),
        name="moba_prompt",
    )(q, kb, vt, kmean.reshape(batch, nb, HW))


def _head_rows(x, t):
    tiled = jnp.concatenate([x] * HEADS, axis=0)
    r = lax.broadcasted_iota(I32, tiled.shape, 0)
    c = lax.broadcasted_iota(I32, tiled.shape, 1)
    return jnp.where(r // t == c // DH, tiled, 0.0)


def _moba_sample_scores_kernel(pt_ref, q_ref, kn_ref, *rest, t, n_pages, page):
    del pt_ref
    pps = PAGES_PER_STEP
    k_refs = rest[:pps]
    pp_ref, po_ref, s_ref, ksum_ref = rest[pps:]
    c = pl.program_id(1)
    nc = pl.num_programs(1)
    ppb = MOBA_BLOCK // page
    n_blk = n_pages // ppb
    scale = DH ** -0.5
    qrows = _head_rows(q_ref[...], t)
    qbf = qrows.astype(BF16)

    for jb in range(pps // ppb):
        ksum = None
        for jp in range(ppb):
            pg = jb * ppb + jp
            kp = k_refs[pg][0]
            s_ref[c * pps + pg] = _mm(qbf, kp, NT_DIMS) * scale
            part = jnp.sum(kp, axis=0, keepdims=True)
            ksum = part if ksum is None else ksum + part
        ksum_ref[pl.ds(c * (pps // ppb) + jb, 1), :] = ksum

    @pl.when(c == nc - 1)
    def _():
        kmean = ksum_ref[...] * (1.0 / MOBA_BLOCK)
        gate = _mm3(qrows, kmean, NT_DIMS)
        bcol = lax.broadcasted_iota(I32, gate.shape, 1)
        selt = jnp.zeros(gate.shape, F32)
        for idx, val in _top_picks(gate, MOBA_TOPK, bcol, n_blk, 1):
            selt = jnp.where((bcol == idx) & (val > NEG_INF), 1.0, selt)
        s_own = _mm(qbf, kn_ref[...], NT_DIMS) * scale
        r = lax.broadcasted_iota(I32, s_own.shape, 0)
        cc = lax.broadcasted_iota(I32, s_own.shape, 1)
        s_own = jnp.where(cc <= r % t, s_own, NEG_INF)
        m = jnp.max(s_own, axis=1, keepdims=True)
        for pg in range(n_pages):
            on = selt[:, pg // ppb:pg // ppb + 1] > 0.5
            m = jnp.maximum(m, jnp.max(jnp.where(on, s_ref[pg], NEG_INF), axis=1, keepdims=True))
        p_own = jnp.exp(s_own - m)
        l = jnp.sum(p_own, axis=1, keepdims=True)
        for pg in range(n_pages):
            on = selt[:, pg // ppb:pg // ppb + 1] > 0.5
            p = jnp.where(on, jnp.exp(s_ref[pg] - m), 0.0)
            s_ref[pg] = p
            l = l + jnp.sum(p, axis=1, keepdims=True)
        inv = 1.0 / l
        po_ref[0] = p_own * inv
        for pg in range(n_pages):
            pp_ref[0, pg] = s_ref[pg] * inv


def _moba_sample_apply_kernel(pt_ref, pp_ref, po_ref, vn_ref, *rest, t):
    del pt_ref
    pps = PAGES_PER_STEP
    v_refs = rest[:pps]
    o_ref, acc_ref = rest[pps:]
    c = pl.program_id(1)
    nc = pl.num_programs(1)

    @pl.when(c == 0)
    def _():
        acc_ref[...] = _mm(po_ref[0], vn_ref[...])

    acc = acc_ref[...]
    for pg in range(pps):
        acc = acc + _mm(pp_ref[0, pg], v_refs[pg][0])
    acc_ref[...] = acc

    @pl.when(c == nc - 1)
    def _():
        for h in range(HEADS):
            o_ref[:, h * DH:(h + 1) * DH] = acc_ref[h * t:(h + 1) * t, h * DH:(h + 1) * DH]


def _moba_sample(q, k_new, v_new, cache_k, cache_v, page_table):
    bd, n_pages = page_table.shape
    n = q.shape[0]
    t = n // bd
    page = cache_k.shape[1]
    pps = PAGES_PER_STEP
    assert n_pages % pps == 0 and (n_pages * page) % MOBA_BLOCK == 0 and MOBA_BLOCK % page == 0
    nc = n_pages // pps
    ht = HEADS * t
    pt = page_table.reshape(-1).astype(I32)
    seq = pl.BlockSpec((t, HW), lambda b, c, pt: (b, 0))

    def page_spec(i):
        return pl.BlockSpec((1, page, HW), lambda b, c, pt: (pt[b * n_pages + c * pps + i], 0, 0))

    p_past, p_own = pl.pallas_call(
        functools.partial(_moba_sample_scores_kernel, t=t, n_pages=n_pages, page=page),
        grid_spec=pltpu.PrefetchScalarGridSpec(
            num_scalar_prefetch=1,
            grid=(bd, nc),
            in_specs=[seq, seq] + [page_spec(i) for i in range(pps)],
            out_specs=[pl.BlockSpec((1, n_pages, ht, page), lambda b, c, pt: (b, 0, 0, 0)),
                       pl.BlockSpec((1, ht, t), lambda b, c, pt: (b, 0, 0))],
            scratch_shapes=[pltpu.VMEM((n_pages, ht, page), F32),
                            pltpu.VMEM((n_pages * page // MOBA_BLOCK, HW), F32)]),
        out_shape=[jax.ShapeDtypeStruct((bd, n_pages, ht, page), F32),
                   jax.ShapeDtypeStruct((bd, ht, t), F32)],
        compiler_params=_params("parallel", "arbitrary"),
        name="moba_sample_scores",
    )(pt, q, k_new, *([cache_k] * pps))

    return pl.pallas_call(
        functools.partial(_moba_sample_apply_kernel, t=t),
        grid_spec=pltpu.PrefetchScalarGridSpec(
            num_scalar_prefetch=1,
            grid=(bd, nc),
            in_specs=[pl.BlockSpec((1, pps, ht, page), lambda b, c, pt: (b, c, 0, 0)),
                      pl.BlockSpec((1, ht, t), lambda b, c, pt: (b, 0, 0)),
                      seq] + [page_spec(i) for i in range(pps)],
            out_specs=seq,
            scratch_shapes=[pltpu.VMEM((ht, HW), F32)]),
        out_shape=jax.ShapeDtypeStruct((n, HW), F32),
        compiler_params=_params("parallel", "arbitrary"),
        name="moba_sample_apply",
    )(pt, p_past, p_own, v_new, *([cache_v] * pps))


def _postmix_kernel(dn_ref, mb_ref, x_ref, wo1_ref, wo2_ref, nx_ref, wq_ref, mk_ref, mv_ref, wxo_ref,
                    nf_ref, wr_ref, br_ref, tri_ref,
                    x2_ref, h2_ref, tope_ref, gate_ref, rank_ref, cnt_ref, carry_ref, *, seqs, n_exp):
    i = pl.program_id(0)

    @pl.when(i == 0)
    def _():
        carry_ref[...] = jnp.zeros_like(carry_ref)

    x1 = x_ref[...] + _mm(dn_ref[...], wo1_ref[...]) + _mm(mb_ref[...], wo2_ref[...])
    tm = x1.shape[0]
    q = _mm(_rms(x1, nx_ref[...]), wq_ref[...])
    scale = DH ** -0.5
    heads = []
    for h in range(HEADS):
        hs = slice(h * DH, (h + 1) * DH)
        if seqs == 1:
            s = _mm(q[:, hs], mk_ref[0, :, hs], NT_DIMS) * scale
            p = jnp.exp(s - jnp.max(s, axis=-1, keepdims=True))
            p = p / jnp.sum(p, axis=-1, keepdims=True)
            heads.append(_mm(p, mv_ref[0, :, hs]))
        else:
            q3 = q[:, hs].reshape(seqs, tm // seqs, DH).astype(BF16)
            s = jnp.einsum("gtd,gmd->gtm", q3, mk_ref[:, :, hs].astype(BF16), preferred_element_type=F32) * scale
            p = jnp.exp(s - jnp.max(s, axis=-1, keepdims=True))
            p = p / jnp.sum(p, axis=-1, keepdims=True)
            o = jnp.einsum("gtm,gmd->gtd", p.astype(BF16), mv_ref[:, :, hs].astype(BF16), preferred_element_type=F32)
            heads.append(o.reshape(tm, DH))
    x2 = x1 + _mm(jnp.concatenate(heads, axis=1), wxo_ref[...])
    x2_ref[...] = x2
    h2 = _rms(x2, nf_ref[...])
    h2_ref[...] = h2

    logits = _mm3(h2, wr_ref[...]) + br_ref[...]
    lt = logits.T[:n_exp]
    erow = lax.broadcasted_iota(I32, lt.shape, 0)
    picks = _top_picks(lt, TOP_K, erow, n_exp, 0)
    ex = [jnp.exp(val - picks[0][1]) for _, val in picks]
    den = ex[0]
    for e in ex[1:]:
        den = den + e
    onehot = [erow == idx for idx, _ in picks]
    any_hot = onehot[0]
    for oh in onehot[1:]:
        any_hot = any_hot | oh
    hot = any_hot.astype(F32)
    base = _mm(hot, tri_ref[...]) + carry_ref[:, 0:1]
    for k in range(TOP_K):
        tope_ref[k:k + 1, :] = picks[k][0]
        gate_ref[k:k + 1, :] = ex[k] / den
        rank_ref[k:k + 1, :] = jnp.sum(jnp.where(onehot[k], base, 0.0), axis=0, keepdims=True).astype(I32)
    pad = SUBLANES - TOP_K
    tope_ref[TOP_K:, :] = jnp.zeros((pad, tm), I32)
    gate_ref[TOP_K:, :] = jnp.zeros((pad, tm), F32)
    rank_ref[TOP_K:, :] = jnp.zeros((pad, tm), I32)
    carry = carry_ref[...] + jnp.sum(hot, axis=1, keepdims=True)
    carry_ref[...] = carry
    cnt_ref[...] = carry.astype(I32)


def _post_mix(dn, mb, x, mem_k, mem_v, w_out, norm_x, w_xq, w_xo, norm_ffn, w_router, b_router):
    n, d = x.shape
    n_seq, n_mem, _ = mem_k.shape
    rows_per_seq = n // n_seq
    n_exp = w_router.shape[1]
    if rows_per_seq >= ROW_TILE:
        tm, seqs = ROW_TILE, 1
        assert rows_per_seq % tm == 0
        mem_map = lambda i: (i * tm // rows_per_seq, 0, 0)
    else:
        seqs = min(max(SAMPLE_SEQS, LANES // rows_per_seq), n_seq)
        tm = seqs * rows_per_seq
        mem_map = lambda i: (i, 0, 0)
    wo = w_out.astype(BF16)
    wr = jnp.pad(w_router, ((0, 0), (0, LANES - n_exp)))
    br = jnp.pad(b_router, (0, LANES - n_exp)).reshape(1, LANES)
    tri = (jnp.arange(tm)[:, None] < jnp.arange(tm)[None, :]).astype(BF16)
    row = lambda w: pl.BlockSpec((tm, w), lambda i: (i, 0))
    col = pl.BlockSpec((SUBLANES, tm), lambda i: (0, i))
    mem_spec = pl.BlockSpec((seqs, n_mem, HW), mem_map)
    return pl.pallas_call(
        functools.partial(_postmix_kernel, seqs=seqs, n_exp=n_exp),
        grid=(n // tm,),
        in_specs=[row(HW), row(HW), row(d), _full((HW, d)), _full((HW, d)), _full((1, d)), _full((d, HW)),
                  mem_spec, mem_spec, _full((HW, d)), _full((1, d)), _full((d, LANES)), _full((1, LANES)),
                  _full((tm, tm))],
        out_specs=[row(d), row(d), col, col, col, _full((n_exp, LANES))],
        out_shape=[jax.ShapeDtypeStruct((n, d), F32), jax.ShapeDtypeStruct((n, d), F32),
                   jax.ShapeDtypeStruct((SUBLANES, n), I32), jax.ShapeDtypeStruct((SUBLANES, n), F32),
                   jax.ShapeDtypeStruct((SUBLANES, n), I32), jax.ShapeDtypeStruct((n_exp, LANES), I32)],
        scratch_shapes=[pltpu.VMEM((n_exp, LANES), F32)],
        compiler_params=_params("arbitrary"),
        name="post_mix",
    )(dn, mb, x, wo[:HW], wo[HW:], norm_x.reshape(1, d), w_xq.astype(BF16), mem_k, mem_v,
      w_xo.astype(BF16), norm_ffn.reshape(1, d), wr, br, tri)


def _plan_kernel(cnt_ref, tope_ref, rank_ref, dest_ref, be_ref, nused_ref, *, n_exp, bm, n_blocks):
    tope = tope_ref[...]
    dest = rank_ref[...]
    start = jnp.int32(0)
    first = pl.program_id(0) == 0
    for e in range(n_exp):
        dest = dest + jnp.where(tope == e, start * bm, 0)
        nb_e = (cnt_ref[e] + (bm - 1)) // bm

        @pl.when(first)
        def _(e=e, start=start, nb_e=nb_e):
            def fill(j, c):
                be_ref[start + j] = e
                return c
            lax.fori_loop(0, nb_e, fill, 0)

        start = start + nb_e
    dest_ref[...] = dest

    @pl.when(first)
    def _():
        nused_ref[0] = start

        def fill(j, c):
            be_ref[j] = n_exp - 1
            return c
        lax.fori_loop(start, n_blocks, fill, 0)


def _moe_plan(cnt, tope, rank, bm, n_blocks):
    n = tope.shape[1]
    n_exp = cnt.shape[0]
    tl = min(2048, n)
    col = pl.BlockSpec((SUBLANES, tl), lambda i: (0, i))
    smem = lambda: pl.BlockSpec(memory_space=pltpu.SMEM)
    return pl.pallas_call(
        functools.partial(_plan_kernel, n_exp=n_exp, bm=bm, n_blocks=n_blocks),
        grid=(n // tl,),
        in_specs=[smem(), col, col],
        out_specs=[col, smem(), smem()],
        out_shape=[jax.ShapeDtypeStruct((SUBLANES, n), I32), jax.ShapeDtypeStruct((n_blocks,), I32),
                   jax.ShapeDtypeStruct((1,), I32)],
        compiler_params=_params("arbitrary"),
        name="moe_plan",
    )(cnt[:, 0], tope, rank)


def _row_copy(src, src_row, dst, dst_row, sem):
    return pltpu.make_async_copy(src.at[pl.ds(src_row, 1), :], dst.at[pl.ds(dst_row, 1), :], sem)


def _dispatch_kernel(dest_ref, x_ref, xs_in_ref, xs_ref, sem, *, tm):
    del xs_in_ref

    def issue(t, c):
        for k in range(TOP_K):
            _row_copy(x_ref, t, xs_ref, dest_ref[k, t], sem).start()
        return c

    lax.fori_loop(0, tm, issue, 0)

    def drain(t, c):
        for k in range(TOP_K):
            _row_copy(x_ref, t, xs_ref, dest_ref[k, t], sem).wait()
        return c

    lax.fori_loop(0, tm, drain, 0)


def _moe_dispatch(h2, dest, n_slots):
    n, d = h2.shape
    tm = min(MOVE_TILE, n)
    return pl.pallas_call(
        functools.partial(_dispatch_kernel, tm=tm),
        grid=(n // tm,),
        in_specs=[pl.BlockSpec((SUBLANES, tm), lambda i: (0, i), memory_space=pltpu.SMEM),
                  pl.BlockSpec((tm, d), lambda i: (i, 0)),
                  pl.BlockSpec(memory_space=pl.ANY)],
        out_specs=pl.BlockSpec(memory_space=pl.ANY),
        out_shape=jax.ShapeDtypeStruct((n_slots, d), F32),
        scratch_shapes=[pltpu.SemaphoreType.DMA],
        input_output_aliases={2: 0},
        compiler_params=_params("arbitrary"),
        name="moe_dispatch",
    )(dest, h2, jnp.zeros((n_slots, d), F32))


def _expert_kernel(be_ref, nused_ref, x_ref, wg_ref, bg_ref, wu_ref, bu_ref, wd_ref, bd_ref, y_ref):
    del be_ref
    i = pl.program_id(0)

    @pl.when(i < nused_ref[0])
    def _():
        x = x_ref[...].astype(BF16)
        g = jnp.minimum(_mm(x, wg_ref[0]) + bg_ref[0], SWIGLU_LIMIT)
        u = jnp.clip(_mm(x, wu_ref[0]) + bu_ref[0], -SWIGLU_LIMIT, SWIGLU_LIMIT)
        a = g * _sigmoid(SWIGLU_ALPHA * g) * (u + 1.0)
        y_ref[...] = _mm(a, wd_ref[0]) + bd_ref[0]

    @pl.when(i >= nused_ref[0])
    def _():
        y_ref[...] = jnp.zeros_like(y_ref)


def _moe_experts(xs, block_e, n_used, w_gate, b_gate, w_up, b_up, w_down, b_down, bm):
    n_slots, d = xs.shape
    n_exp, _, d_ff = w_gate.shape
    n_blocks = n_slots // bm
    wspec = lambda r, c: pl.BlockSpec((1, r, c), lambda i, be, nu: (be[i], 0, 0))
    return pl.pallas_call(
        _expert_kernel,
        grid_spec=pltpu.PrefetchScalarGridSpec(
            num_scalar_prefetch=2,
            grid=(n_blocks,),
            in_specs=[pl.BlockSpec((bm, d), lambda i, be, nu: (i, 0)),
                      wspec(d, d_ff), wspec(1, d_ff), wspec(d, d_ff), wspec(1, d_ff), wspec(d_ff, d), wspec(1, d)],
            out_specs=pl.BlockSpec((bm, d), lambda i, be, nu: (i, 0))),
        out_shape=jax.ShapeDtypeStruct((n_slots, d), F32),
        compiler_params=_params("arbitrary"),
        name="moe_experts",
    )(block_e, n_used, xs, w_gate.astype(BF16), b_gate.reshape(n_exp, 1, d_ff), w_up.astype(BF16),
      b_up.reshape(n_exp, 1, d_ff), w_down.astype(BF16), b_down.reshape(n_exp, 1, d))


def _combine_kernel(dest_ref, gate_ref, x_ref, nw_ref, ys_ref, o_ref, buf_ref, sem, *, tm):
    def issue(t, c):
        for k in range(TOP_K):
            _row_copy(ys_ref, dest_ref[k, t], buf_ref.at[k], t, sem).start()
        return c

    lax.fori_loop(0, tm, issue, 0)

    def drain(t, c):
        for k in range(TOP_K):
            _row_copy(ys_ref, dest_ref[k, t], buf_ref.at[k], t, sem).wait()
        return c

    lax.fori_loop(0, tm, drain, 0)
    gt = gate_ref[...].T
    acc = x_ref[...]
    for k in range(TOP_K):
        acc = acc + gt[:, k:k + 1] * buf_ref[k]
    o_ref[...] = _rms(acc, nw_ref[...])


def _moe_combine(ys, dest, gates, x2, norm_final):
    n, d = x2.shape
    tm = min(MOVE_TILE, n)
    col = lambda ms: pl.BlockSpec((SUBLANES, tm), lambda i: (0, i), memory_space=ms)
    return pl.pallas_call(
        functools.partial(_combine_kernel, tm=tm),
        grid=(n // tm,),
        in_specs=[col(pltpu.SMEM), col(pltpu.VMEM), pl.BlockSpec((tm, d), lambda i: (i, 0)), _full((1, d)),
                  pl.BlockSpec(memory_space=pl.ANY)],
        out_specs=pl.BlockSpec((tm, d), lambda i: (i, 0)),
        out_shape=jax.ShapeDtypeStruct((n, d), F32),
        scratch_shapes=[pltpu.VMEM((TOP_K, tm, d), F32), pltpu.SemaphoreType.DMA],
        compiler_params=_params("arbitrary"),
        name="moe_combine",
    )(dest, gates, x2, norm_final.reshape(1, d), ys)


def _moe_and_final_norm(x2, h2, tope, gates, rank, cnt, w_gate, b_gate, w_up, b_up, w_down, b_down, norm_final):
    n = x2.shape[0]
    n_exp = w_gate.shape[0]
    bm = MOE_BLOCK_ROWS
    n_blocks = -(-(n * TOP_K + n_exp * (bm - 1)) // bm)
    dest, block_e, n_used = _moe_plan(cnt, tope, rank, bm, n_blocks)
    xs = _moe_dispatch(h2, dest, n_blocks * bm)
    ys = _moe_experts(xs, block_e, n_used, w_gate, b_gate, w_up, b_up, w_down, b_down, bm)
    return _moe_combine(ys, dest, gates, x2, norm_final)


def kernel(x_prompt, x_sample, mem_prompt, cache_k, cache_v, page_table, state_delta, state_conv, cache_mem_k, cache_mem_v, norm_mix, w_in, conv_w, a_log, dt_bias, dn_norm, w_out, norm_x, norm_mem, w_xq, w_xk, w_xv, w_xo, norm_ffn, w_router, b_router, w_gate, b_gate, w_up, b_up, w_down, b_down, norm_final):
    depth = w_in.shape[0]
    assert depth == 1, "single-layer stack"
    b, s, d = x_prompt.shape
    bd, t, _ = x_sample.shape
    n_mem = mem_prompt.shape[1]
    n_pool, page = cache_k.shape[1], cache_k.shape[2]
    past = page_table.shape[1] * page
    assert past % MOBA_BLOCK == 0, "past length must be whole MoBA blocks"
    l = 0
    moe_w = (w_gate[l], b_gate[l], w_up[l], b_up[l], w_down[l], b_down[l])

    xp = x_prompt.reshape(b * s, d)
    cos_p, sin_p = _rope_tables(jnp.arange(s))
    conv_in, z, gb, q, k, v, kb, vt, kmean = _in_proj(xp, norm_mix[l], w_in[l], a_log[l], dt_bias[l],
                                                      cos_p, sin_p, prompt=True)
    dn, s_p, cv_p = _deltanet_prompt(conv_in, z, gb, conv_w[l], dn_norm[l], b)
    mb = _moba_prompt(q, kb, vt, kmean, b)
    mk_p, mv_p = _mem_kv(mem_prompt.reshape(b * n_mem, d), norm_mem[l], w_xk[l], w_xv[l])
    x2, h2, tope, gates, rank, cnt = _post_mix(
        dn, mb, xp, mk_p.reshape(b, n_mem, HW), mv_p.reshape(b, n_mem, HW), w_out[l], norm_x[l], w_xq[l],
        w_xo[l], norm_ffn[l], w_router[l], b_router[l])
    y_prompt = _moe_and_final_norm(x2, h2, tope, gates, rank, cnt, *moe_w, norm_final)

    xs = x_sample.reshape(bd * t, d)
    tile_s = min(ROW_TILE, bd * t)
    cos_s, sin_s = _rope_tables(past + jnp.arange(tile_s) % t)
    conv_s, z_s, gb_s, q_s, k_s, v_s = _in_proj(xs, norm_mix[l], w_in[l], a_log[l], dt_bias[l],
                                                cos_s, sin_s, prompt=False)
    dn_s, s_s, cv_s = _deltanet_sample(conv_s, z_s, gb_s, state_conv[l], state_delta[l], conv_w[l], dn_norm[l])
    mb_s = _moba_sample(q_s, k_s, v_s, cache_k.reshape(n_pool, page, HW), cache_v.reshape(n_pool, page, HW),
                        page_table)
    x2s, h2s, tope_s, gates_s, rank_s, cnt_s = _post_mix(
        dn_s, mb_s, xs, cache_mem_k.reshape(bd, n_mem, HW), cache_mem_v.reshape(bd, n_mem, HW),
        w_out[l], norm_x[l], w_xq[l], w_xo[l], norm_ffn[l], w_router[l], b_router[l])
    y_sample = _moe_and_final_norm(x2s, h2s, tope_s, gates_s, rank_s, cnt_s, *moe_w, norm_final)

    hd = (HEADS, DH)
    return (y_prompt.reshape(b, s, d), y_sample.reshape(bd, t, d),
            k.reshape(1, b, s, *hd), v.reshape(1, b, s, *hd),
            k_s.reshape(1, bd, t, *hd), v_s.reshape(1, bd, t, *hd),
            s_p[None], s_s[None], cv_p[None], cv_s[None],
            mk_p.reshape(1, b, n_mem, *hd), mv_p.reshape(1, b, n_mem, *hd))
```

```python
import functools
import math

import jax
import jax.numpy as jnp
from jax import lax
from jax.experimental import pallas as pl
from jax.experimental.pallas import tpu as pltpu

F32 = jnp.float32
BF16 = jnp.bfloat16
I32 = jnp.int32

HEADS = 4
DH = 128
HW = HEADS * DH
CONV_W = 4
DN_CHUNK = 64
MOBA_BLOCK = 256
MOBA_TOPK = 3
ROPE_THETA = 10000.0
TOP_K = 4
SWIGLU_ALPHA = 1.702
SWIGLU_LIMIT = 7.0
EPS = 1e-6
NEG_INF = float("-inf")
LOG2_E = math.log2(math.e)

LANES = 128
SUBLANES = 8
VMEM_LIMIT = 56 * 1024 * 1024

ROW_TILE = 512
DN_GROUP = 256
MOBA_GROUP = 4
MOE_BLOCK_ROWS = 512
MOVE_TILE = 256
PAGES_PER_STEP = 16
SAMPLE_SEQS = 8

NN_DIMS = (((1,), (0,)), ((), ()))
NT_DIMS = (((1,), (1,)), ((), ()))
TN_DIMS = (((0,), (0,)), ((), ()))


def _mm(a, b, dims=NN_DIMS):
    return lax.dot_general(a.astype(BF16), b.astype(BF16), dims, preferred_element_type=F32)


def _split(a):
    hi = a.astype(BF16)
    lo = (a - hi.astype(F32)).astype(BF16)
    return hi, lo


def _mm3(a, b, dims=NN_DIMS):
    a_hi, a_lo = _split(a)
    b_hi, b_lo = _split(b)
    d = functools.partial(lax.dot_general, dimension_numbers=dims, preferred_element_type=F32)
    return d(a_hi, b_hi) + (d(a_hi, b_lo) + d(a_lo, b_hi))


def _rms(x, w):
    return x * lax.rsqrt(jnp.mean(x * x, axis=-1, keepdims=True) + EPS) * w


def _sigmoid(x):
    return 1.0 / (1.0 + jnp.exp(-x))


def _softplus(x):
    return jnp.maximum(x, 0.0) + jnp.log1p(jnp.exp(-jnp.abs(x)))


def _params(*sem):
    return pltpu.CompilerParams(dimension_semantics=sem, vmem_limit_bytes=VMEM_LIMIT)


def _full(shape):
    nd = len(shape)
    return pl.BlockSpec(shape, lambda *_: (0,) * nd)


def _top_picks(scores, n_pick, ids, n_ids, axis):
    picks = []
    cur = scores
    for _ in range(n_pick):
        m = jnp.max(cur, axis=axis, keepdims=True)
        idx = jnp.min(jnp.where(cur == m, ids, n_ids), axis=axis, keepdims=True)
        picks.append((idx, m))
        cur = jnp.where(ids == idx, NEG_INF, cur)
    return picks


def _memkv_kernel(m_ref, nw_ref, wk_ref, wv_ref, k_ref, v_ref):
    h = _rms(m_ref[...], nw_ref[...])
    k_ref[...] = _mm(h, wk_ref[...])
    v_ref[...] = _mm(h, wv_ref[...])


def _mem_kv(mem, norm_mem, w_xk, w_xv):
    n, d = mem.shape
    tm = min(ROW_TILE, n)
    return pl.pallas_call(
        _memkv_kernel,
        grid=(n // tm,),
        in_specs=[pl.BlockSpec((tm, d), lambda i: (i, 0)), _full((1, d)), _full(w_xk.shape), _full(w_xv.shape)],
        out_specs=[pl.BlockSpec((tm, HW), lambda i: (i, 0))] * 2,
        out_shape=[jax.ShapeDtypeStruct((n, HW), F32)] * 2,
        compiler_params=_params("parallel"),
        name="mem_kv",
    )(mem, norm_mem.reshape(1, d), w_xk.astype(BF16), w_xv.astype(BF16))


def _rope(x, cos, sin):
    return x * cos + pltpu.roll(x, DH // 2, axis=1) * sin


def _inproj_kernel(x_ref, nw_ref, w_ref, wab_ref, cos_ref, sin_ref, alog_ref, dtb_ref,
                   conv_ref, z_ref, gb_ref, q_ref, k_ref, v_ref, *rest, conv_ch, prompt):
    x = x_ref[...]
    hb = _rms(x, nw_ref[...]).astype(BF16)
    conv_ref[...] = _mm(hb, w_ref[:, 0:conv_ch])
    z_ref[...] = _mm(hb, w_ref[:, conv_ch:conv_ch + HW])
    ab = _mm(hb, wab_ref[...])
    lane = lax.broadcasted_iota(I32, ab.shape, 1)
    g = -jnp.exp(alog_ref[...]) * _softplus(ab + dtb_ref[...])
    gb_ref[...] = jnp.where(lane < HEADS, g, _sigmoid(ab))
    off = conv_ch + HW
    cos = cos_ref[...]
    sin = sin_ref[...]
    q = _mm(hb, w_ref[:, off:off + HW])
    k = _mm(hb, w_ref[:, off + HW:off + 2 * HW])
    v = _mm(hb, w_ref[:, off + 2 * HW:off + 3 * HW])
    k_rot = []
    for h in range(HEADS):
        sl = slice(h * DH, (h + 1) * DH)
        q_ref[:, sl] = _rope(q[:, sl], cos, sin)
        k_rot.append(_rope(k[:, sl], cos, sin))
        k_ref[:, sl] = k_rot[h]
    v_ref[...] = v
    if prompt:
        kb_ref, vt_ref, kmean_ref = rest
        for h in range(HEADS):
            kb_ref[:, h * DH:(h + 1) * DH] = k_rot[h].astype(BF16)
        for blk in range(x.shape[0] // MOBA_BLOCK):
            rows = slice(blk * MOBA_BLOCK, (blk + 1) * MOBA_BLOCK)
            vt_ref[blk] = v[rows].T.astype(BF16)
            for h in range(HEADS):
                kmean_ref[blk, :, h * DH:(h + 1) * DH] = jnp.mean(k_rot[h][rows], axis=0, keepdims=True)


def _in_proj(x, norm_w, w_in, a_log, dt_bias, cos, sin, *, prompt):
    n, d = x.shape
    conv_ch = 3 * HW
    tm = min(ROW_TILE, n)
    off_a = conv_ch + HW
    w_main = jnp.concatenate([w_in[:, :off_a], w_in[:, off_a + 2 * HEADS:]], axis=1).astype(BF16)
    w_ab = jnp.pad(w_in[:, off_a:off_a + 2 * HEADS], ((0, 0), (0, LANES - 2 * HEADS))).astype(BF16)
    alog = jnp.pad(a_log, (0, LANES - HEADS)).reshape(1, LANES)
    dtb = jnp.pad(dt_bias, (0, LANES - HEADS)).reshape(1, LANES)
    n_pos = cos.shape[0] // tm
    row = lambda w: pl.BlockSpec((tm, w), lambda i: (i, 0))
    out_specs = [row(conv_ch), row(HW), row(LANES), row(HW), row(HW), row(HW)]
    out_shape = [jax.ShapeDtypeStruct((n, w), F32) for w in (conv_ch, HW, LANES, HW, HW, HW)]
    if prompt:
        nb = tm // MOBA_BLOCK
        out_specs += [row(HW),
                      pl.BlockSpec((nb, HW, MOBA_BLOCK), lambda i: (i, 0, 0)),
                      pl.BlockSpec((nb, 1, HW), lambda i: (i, 0, 0))]
        out_shape += [jax.ShapeDtypeStruct((n, HW), BF16),
                      jax.ShapeDtypeStruct((n // MOBA_BLOCK, HW, MOBA_BLOCK), BF16),
                      jax.ShapeDtypeStruct((n // MOBA_BLOCK, 1, HW), F32)]
    return pl.pallas_call(
        functools.partial(_inproj_kernel, conv_ch=conv_ch, prompt=prompt),
        grid=(n // tm,),
        in_specs=[row(d), _full((1, d)), _full(w_main.shape), _full(w_ab.shape),
                  pl.BlockSpec((tm, DH), lambda i: (i % n_pos, 0)),
                  pl.BlockSpec((tm, DH), lambda i: (i % n_pos, 0)),
                  _full((1, LANES)), _full((1, LANES))],
        out_specs=out_specs,
        out_shape=out_shape,
        compiler_params=_params("parallel"),
        name="in_proj",
    )(x, norm_w.reshape(1, d), w_main, w_ab, cos, sin, alog, dtb)


def _rope_tables(pos):
    half = DH // 2
    inv = ROPE_THETA ** (-jnp.arange(half, dtype=F32) / half)
    ang = pos.astype(F32)[:, None] * inv[None, :]
    c, s = jnp.cos(ang), jnp.sin(ang)
    return jnp.concatenate([c, c], axis=1), jnp.concatenate([-s, s], axis=1)


def _chunk_cumsum(g, row_in_chunk):
    for sh in (1, 2, 4, 8, 16, 32):
        g = g + jnp.where(row_in_chunk >= sh, pltpu.roll(g, sh, axis=0), 0.0)
    return g


def _unit_lower_inverse(a):
    n = a.shape[0]
    eye = (lax.broadcasted_iota(I32, (n, n), 0) == lax.broadcasted_iota(I32, (n, n), 1)).astype(F32)
    p = eye - a
    ak = a
    for _ in range(5):
        ak = _mm(ak, ak)
        p = p + _mm(p, ak)
    return p


def _gated_out(o, z, dn_w):
    return _rms(o, dn_w) * (z * _sigmoid(z))


def _dn_prompt_kernel(u_ref, halo_ref, z_ref, gb_ref, cw_ref, dnw_ref,
                      o_ref, sfin_ref, cnew_ref,
                      ext_ref, s_ref, wq_ref, us_ref, kd_ref, vn_ref, oi_ref, qk_ref, gl_ref, *, tm):
    i = pl.program_id(1)
    nt = pl.num_programs(1)
    nc = tm // DN_CHUNK
    cpg = DN_GROUP // DN_CHUNK
    hal = CONV_W - 1

    @pl.when(i == 0)
    def _():
        s_ref[...] = jnp.zeros_like(s_ref)

    u = u_ref[...]
    ext_ref[0:SUBLANES, :] = jnp.where(i > 0, halo_ref[...], 0.0)
    ext_ref[SUBLANES:SUBLANES + tm, :] = u
    cw = cw_ref[...]
    y = u * cw[hal:hal + 1]
    for j in range(hal):
        y = y + ext_ref[SUBLANES - hal + j:SUBLANES - hal + j + tm, :] * cw[j:j + 1]
    qkv = y * _sigmoid(y)

    gb = gb_ref[...]
    row_in_chunk = lax.broadcasted_iota(I32, gb.shape, 0) % DN_CHUNK
    gc = _chunk_cumsum(gb, row_in_chunk)
    gc3 = gc.reshape(nc, DN_CHUNK, LANES)
    glast = jnp.broadcast_to(gc3[:, DN_CHUNK - 1:DN_CHUNK, :], gc3.shape).reshape(tm, LANES)
    gl_ref[...] = jnp.exp(glast)
    gct = gc.T

    gi = lax.broadcasted_iota(I32, (DN_GROUP, DN_GROUP), 0)
    gj = lax.broadcasted_iota(I32, (DN_GROUP, DN_GROUP), 1)
    same = (gi // DN_CHUNK) == (gj // DN_CHUNK)
    causal = same & (gi >= gj)
    strict = same & (gi > gj)

    for h in range(HEADS):
        qh = qkv[:, h * DH:(h + 1) * DH]
        kh = qkv[:, HW + h * DH:HW + (h + 1) * DH]
        vh = qkv[:, 2 * HW + h * DH:2 * HW + (h + 1) * DH]
        qn = qh * lax.rsqrt(jnp.sum(qh * qh, axis=-1, keepdims=True) + EPS) * (DH ** -0.5)
        kn = kh * lax.rsqrt(jnp.sum(kh * kh, axis=-1, keepdims=True) + EPS)
        gcol = gc[:, h:h + 1]
        beta = gb[:, HEADS + h:HEADS + h + 1]
        eg = jnp.exp(gcol)
        kbeta = kn * beta
        kd_ref[h] = kn * jnp.exp(glast[:, h:h + 1] - gcol)
        qe = qn * eg
        for r in range(tm // DN_GROUP):
            rs = slice(r * DN_GROUP, (r + 1) * DN_GROUP)
            decay = jnp.where(causal, jnp.exp(jnp.minimum(gcol[rs] - gct[h:h + 1, rs], 0.0)), 0.0)
            a = jnp.where(strict, _mm(kbeta[rs], kn[rs], NT_DIMS) * decay, 0.0)
            qk_ref[h, rs, :] = _mm(qn[rs], kn[rs], NT_DIMS) * decay
            t_inv = _unit_lower_inverse(a)
            rhs = jnp.concatenate([vh[rs] * beta[rs], kbeta[rs] * eg[rs]], axis=1)
            uw = _mm(t_inv, rhs)
            us_ref[h, rs, :] = uw[:, :DH]
            for c in range(cpg):
                lo = r * DN_GROUP + c * DN_CHUNK
                wq_ref[h, r * cpg + c, 0:DN_CHUNK, :] = uw[c * DN_CHUNK:(c + 1) * DN_CHUNK, DH:]
                wq_ref[h, r * cpg + c, DN_CHUNK:2 * DN_CHUNK, :] = qe[lo:lo + DN_CHUNK]

    def chunk_step(c, carry):
        r0 = pl.multiple_of(c * DN_CHUNK, DN_CHUNK)
        rows = pl.ds(r0, DN_CHUNK)
        for h in range(HEADS):
            s = s_ref[h]
            m1 = _mm(wq_ref[h, c], s)
            v_new = us_ref[h, rows, :] - m1[:DN_CHUNK]
            oi_ref[h, rows, :] = m1[DN_CHUNK:]
            vn_ref[h, rows, :] = v_new
            s_ref[h] = s * gl_ref[pl.ds(r0, 1), h:h + 1] + _mm(kd_ref[h, rows, :], v_new, TN_DIMS)
        return carry

    lax.fori_loop(0, nc, chunk_step, 0)

    z = z_ref[...]
    dnw = dnw_ref[...]
    for h in range(HEADS):
        parts = []
        for r in range(tm // DN_GROUP):
            rs = slice(r * DN_GROUP, (r + 1) * DN_GROUP)
            parts.append(oi_ref[h, rs, :] + _mm(qk_ref[h, rs, :], vn_ref[h, rs, :]))
        o = jnp.concatenate(parts, axis=0)
        o_ref[:, h * DH:(h + 1) * DH] = _gated_out(o, z[:, h * DH:(h + 1) * DH], dnw).astype(o_ref.dtype)

    @pl.when(i == nt - 1)
    def _():
        sfin_ref[0] = s_ref[...]
        cnew_ref[0] = ext_ref[SUBLANES + tm - hal:SUBLANES + tm, :]


def _deltanet_prompt(conv_in, z, gb, conv_w, dn_norm, batch):
    n, conv_ch = conv_in.shape
    s = n // batch
    tm = min(ROW_TILE, s)
    nt = s // tm
    hal = CONV_W - 1
    hb = tm // SUBLANES
    row = lambda w: pl.BlockSpec((tm, w), lambda b, i: (b * nt + i, 0))
    head_scr = lambda w: pltpu.VMEM((HEADS, tm, w), F32)
    return pl.pallas_call(
        functools.partial(_dn_prompt_kernel, tm=tm),
        grid=(batch, nt),
        in_specs=[row(conv_ch),
                  pl.BlockSpec((SUBLANES, conv_ch), lambda b, i: (jnp.maximum((b * nt + i) * hb - 1, 0), 0)),
                  row(HW), row(LANES), _full((CONV_W, conv_ch)), _full((1, DH))],
        out_specs=[row(HW),
                   pl.BlockSpec((1, HEADS, DH, DH), lambda b, i: (b, 0, 0, 0)),
                   pl.BlockSpec((1, hal, conv_ch), lambda b, i: (b, 0, 0))],
        out_shape=[jax.ShapeDtypeStruct((n, HW), BF16),
                   jax.ShapeDtypeStruct((batch, HEADS, DH, DH), F32),
                   jax.ShapeDtypeStruct((batch, hal, conv_ch), F32)],
        scratch_shapes=[pltpu.VMEM((SUBLANES + tm, conv_ch), F32),
                        pltpu.VMEM((HEADS, DH, DH), F32),
                        pltpu.VMEM((HEADS, tm // DN_CHUNK, 2 * DN_CHUNK, DH), F32),
                        head_scr(DH), head_scr(DH), head_scr(DH), head_scr(DH), head_scr(DN_GROUP),
                        pltpu.VMEM((tm, LANES), F32)],
        compiler_params=_params("arbitrary", "arbitrary"),
        name="deltanet_prompt",
    )(conv_in, conv_in, z, gb, conv_w, dn_norm.reshape(1, DH))


def _dn_sample_kernel(u_ref, cbuf_ref, z_ref, gb_ref, st_ref, cw_ref, dnw_ref,
                      o_ref, snew_ref, cnew_ref, ext_ref, *, t, seqs):
    hal = CONV_W - 1
    cw = cw_ref[...]
    dnw = dnw_ref[...]

    def seq_step(si, carry):
        r0 = pl.multiple_of(si * t, t)
        rows = pl.ds(r0, t)
        ext_ref[SUBLANES - hal:SUBLANES, :] = cbuf_ref[si]
        ext_ref[SUBLANES:SUBLANES + t, :] = u_ref[rows, :]
        y = ext_ref[SUBLANES - hal:SUBLANES - hal + t, :] * cw[0:1]
        for j in range(1, CONV_W):
            y = y + ext_ref[SUBLANES - hal + j:SUBLANES - hal + j + t, :] * cw[j:j + 1]
        cnew_ref[si] = ext_ref[SUBLANES + t - hal:SUBLANES + t, :]
        qkv = y * _sigmoid(y)
        gb = gb_ref[rows, :]
        z = z_ref[rows, :]
        for h in range(HEADS):
            qh = qkv[:, h * DH:(h + 1) * DH]
            kh = qkv[:, HW + h * DH:HW + (h + 1) * DH]
            vh = qkv[:, 2 * HW + h * DH:2 * HW + (h + 1) * DH]
            qn = qh * lax.rsqrt(jnp.sum(qh * qh, axis=-1, keepdims=True) + EPS) * (DH ** -0.5)
            kn = kh * lax.rsqrt(jnp.sum(kh * kh, axis=-1, keepdims=True) + EPS)
            qt = qn.T
            kt = kn.T
            eg = jnp.exp(gb[:, h:h + 1])
            beta = gb[:, HEADS + h:HEADS + h + 1]
            s = st_ref[si, h]
            outs = []
            for ti in range(t):
                kcol = kt[:, ti:ti + 1]
                s = s * eg[ti:ti + 1]
                v_new = beta[ti:ti + 1] * (vh[ti:ti + 1] - jnp.sum(s * kcol, axis=0, keepdims=True))
                s = s + kcol * v_new
                outs.append(jnp.sum(s * qt[:, ti:ti + 1], axis=0, keepdims=True))
            snew_ref[si, h] = s
            o = jnp.concatenate(outs, axis=0)
            o_ref[rows, h * DH:(h + 1) * DH] = _gated_out(o, z[:, h * DH:(h + 1) * DH], dnw)
        return carry

    lax.fori_loop(0, seqs, seq_step, 0)


def _deltanet_sample(conv_in, z, gb, state_conv, state_delta, conv_w, dn_norm):
    n, conv_ch = conv_in.shape
    bd = state_delta.shape[0]
    t = n // bd
    seqs = min(SAMPLE_SEQS, bd)
    hal = CONV_W - 1
    row = lambda w: pl.BlockSpec((seqs * t, w), lambda i: (i, 0))
    st_spec = pl.BlockSpec((seqs, HEADS, DH, DH), lambda i: (i, 0, 0, 0))
    cb_spec = pl.BlockSpec((seqs, hal, conv_ch), lambda i: (i, 0, 0))
    return pl.pallas_call(
        functools.partial(_dn_sample_kernel, t=t, seqs=seqs),
        grid=(bd // seqs,),
        in_specs=[row(conv_ch), cb_spec, row(HW), row(LANES), st_spec, _full((CONV_W, conv_ch)), _full((1, DH))],
        out_specs=[row(HW), st_spec, cb_spec],
        out_shape=[jax.ShapeDtypeStruct((n, HW), F32),
                   jax.ShapeDtypeStruct(state_delta.shape, F32),
                   jax.ShapeDtypeStruct(state_conv.shape, F32)],
        scratch_shapes=[pltpu.VMEM((SUBLANES + t, conv_ch), F32)],
        compiler_params=_params("parallel"),
        name="deltanet_sample",
    )(conv_in, state_conv, z, gb, state_delta, conv_w, dn_norm.reshape(1, DH))


def _moba_prompt_kernel(q_ref, k_ref, vt_ref, km_ref, o_ref, sel_ref, m_ref, l_ref, acc_ref,
                        sa_ref, sb_ref, pa_ref, pb_ref, *, nb):
    qb = pl.program_id(2)
    blk = MOBA_BLOCK
    grp = MOBA_GROUP
    q = q_ref[...]
    qs = (q * (DH ** -0.5 * LOG2_E)).astype(BF16)

    gs = _mm3(km_ref[0], q, NT_DIMS)
    brow = lax.broadcasted_iota(I32, gs.shape, 0)
    gs = jnp.where(brow < qb, gs, NEG_INF)
    sel = jnp.zeros(gs.shape, F32)
    for idx, val in _top_picks(gs, MOBA_TOPK, brow, nb, 0):
        sel = jnp.where((brow == idx) & (val > NEG_INF), 1.0, sel)
    sel_ref[...] = sel

    def scores(row0, rows):
        kj = k_ref[pl.ds(pl.multiple_of(row0, blk), rows), :]
        return lax.dot_general(kj, qs, NT_DIMS, preferred_element_type=F32)

    ki = lax.broadcasted_iota(I32, (blk, blk), 0)
    qi = lax.broadcasted_iota(I32, (blk, blk), 1)
    s = jnp.where(ki <= qi, scores(qb * blk, blk), NEG_INF)
    m = jnp.max(s, axis=0, keepdims=True)
    p = jnp.exp2(s - m)
    m_ref[...] = m
    l_ref[...] = jnp.sum(p, axis=0, keepdims=True)
    acc_ref[...] = _mm(vt_ref[qb], p)

    n_groups = (qb + grp - 1) // grp
    last_group = nb // grp - 1
    sa_ref[...] = scores(0, grp * blk)
    pb_ref[...] = jnp.zeros_like(pb_ref)

    def apply_probs(g, p_ref):
        acc = acc_ref[...]
        for b in range(grp):
            acc = acc + lax.dot_general(vt_ref[g * grp + b], p_ref[b], NN_DIMS, preferred_element_type=F32)
        return acc

    def step(g, s_cur, s_next, p_prev, p_cur):
        s_next[...] = scores(jnp.minimum(g + 1, last_group) * (grp * blk), grp * blk)
        acc = apply_probs(jnp.maximum(g - 1, 0), p_prev)
        j0 = g * grp
        m_old = m_ref[...]
        m_new = m_old
        picked = []
        for b in range(grp):
            on = sel_ref[pl.ds(j0 + b, 1), :] > 0.5
            col_max = jnp.max(s_cur[b * blk:(b + 1) * blk, :], axis=0, keepdims=True)
            m_new = jnp.maximum(m_new, jnp.where(on, col_max, NEG_INF))
            picked.append(on)
        alpha = jnp.exp2(m_old - m_new)
        l = alpha * l_ref[...]
        for b in range(grp):
            p = jnp.exp2(s_cur[b * blk:(b + 1) * blk, :] - jnp.where(picked[b], m_new, -NEG_INF))
            l = l + jnp.sum(p, axis=0, keepdims=True)
            p_cur[b] = p.astype(BF16)
        m_ref[...] = m_new
        l_ref[...] = l
        acc_ref[...] = alpha * acc

    def group_pair(i, carry):
        step(2 * i, sa_ref, sb_ref, pb_ref, pa_ref)
        step(2 * i + 1, sb_ref, sa_ref, pa_ref, pb_ref)
        return carry

    lax.fori_loop(0, n_groups // 2, group_pair, 0)
    odd = n_groups % 2 == 1

    @pl.when(odd)
    def _():
        step(n_groups - 1, sa_ref, sb_ref, pb_ref, pa_ref)
        o_ref[...] = (apply_probs(n_groups - 1, pa_ref) / l_ref[...]).T.astype(o_ref.dtype)

    @pl.when(jnp.logical_not(odd))
    def _():
        o_ref[...] = (apply_probs(jnp.maximum(n_groups - 1, 0), pb_ref) / l_ref[...]).T.astype(o_ref.dtype)


def _moba_prompt(q, kb, vt, kmean, batch):
    n = q.shape[0]
    s = n // batch
    nb = s // MOBA_BLOCK
    assert nb % MOBA_GROUP == 0
    return pl.pallas_call(
        functools.partial(_moba_prompt_kernel, nb=nb),
        grid=(batch, HEADS, nb),
        in_specs=[pl.BlockSpec((MOBA_BLOCK, DH), lambda b, h, j: (b * nb + j, h)),
                  pl.BlockSpec((s, DH), lambda b, h, j: (b, h)),
                  pl.BlockSpec((nb, DH, MOBA_BLOCK), lambda b, h, j: (b, h, 0)),
                  pl.BlockSpec((1, nb, DH), lambda b, h, j: (b, 0, h))],
        out_specs=pl.BlockSpec((MOBA_BLOCK, DH), lambda b, h, j: (b * nb + j, h)),
        out_shape=jax.ShapeDtypeStruct((n, HW), BF16),
        scratch_shapes=[pltpu.VMEM((nb, MOBA_BLOCK), F32), pltpu.VMEM((1, MOBA_BLOCK), F32),
                        pltpu.VMEM((1, MOBA_BLOCK), F32), pltpu.VMEM((DH, MOBA_BLOCK), F32),
                        pltpu.VMEM((MOBA_GROUP * MOBA_BLOCK, MOBA_BLOCK), F32),
                        pltpu.VMEM((MOBA_GROUP * MOBA_BLOCK, MOBA_BLOCK), F32),
                        pltpu.VMEM((MOBA_GROUP, MOBA_BLOCK, MOBA_BLOCK), BF16),
                        pltpu.VMEM((MOBA_GROUP, MOBA_BLOCK, MOBA_BLOCK), BF16)],
        compiler_params=_params("parallel", "parallel", "arbitrary"),
        name="moba_prompt",
    )(q, kb, vt, kmean.reshape(batch, nb, HW))


def _moba_sample_scores_kernel(pt_ref, q_ref, kn_ref, *rest, t, n_pages, page):
    del pt_ref
    pps = PAGES_PER_STEP
    k_refs = rest[:pps]
    pp_ref, po_ref, s_ref, ksum_ref = rest[pps:]
    c = pl.program_id(1)
    nc = pl.num_programs(1)
    ppb = MOBA_BLOCK // page
    n_blk = n_pages // ppb
    fold = SUBLANES // HEADS
    scale = DH ** -0.5
    q = q_ref[...]
    qall = jnp.concatenate([q[:, h * DH:(h + 1) * DH] for h in range(HEADS)], axis=0)
    qbf = (qall * scale).astype(BF16)

    for jb in range(pps // ppb):
        ksum = None
        for jp in range(ppb):
            pg = jb * ppb + jp
            kp = k_refs[pg][0]
            s_ref[c * pps + pg] = _mm(qbf, kp, NT_DIMS)
            part = jnp.sum(kp.reshape(page * HEADS // SUBLANES, SUBLANES, DH), axis=0)
            ksum = part if ksum is None else ksum + part
        total = ksum
        for sh in range(1, fold):
            total = total + pltpu.roll(ksum, sh * HEADS, axis=0)
        ksum_ref[pl.ds(pl.multiple_of((c * (pps // ppb) + jb) * SUBLANES, SUBLANES), SUBLANES), :] = total

    @pl.when(c == nc - 1)
    def _():
        ht = HEADS * t
        kmean = ksum_ref[...] * (1.0 / MOBA_BLOCK)
        gate = _mm3(qall, kmean, NT_DIMS)
        grow = lax.broadcasted_iota(I32, gate.shape, 0)
        gcol = lax.broadcasted_iota(I32, gate.shape, 1)
        gate = jnp.where(gcol % SUBLANES == grow // t, gate, NEG_INF)
        picks = [(idx // SUBLANES, val > NEG_INF)
                 for idx, val in _top_picks(gate, MOBA_TOPK, gcol, n_blk * SUBLANES, 1)]

        def block_on(j):
            on = picks[0][1] & (picks[0][0] == j)
            for blk_id, ok in picks[1:]:
                on = on | (ok & (blk_id == j))
            return on

        s_own = _mm(qbf, kn_ref[...], NT_DIMS)
        ro = lax.broadcasted_iota(I32, s_own.shape, 0)
        co = lax.broadcasted_iota(I32, s_own.shape, 1)
        s_own = jnp.where((co % HEADS == ro // t) & (co // HEADS <= ro % t), s_own, NEG_INF)
        rp = lax.broadcasted_iota(I32, (ht, page * HEADS), 0)
        cp = lax.broadcasted_iota(I32, (ht, page * HEADS), 1)
        same_head = cp % HEADS == rp // t
        m = jnp.max(s_own, axis=1, keepdims=True)
        for pg in range(n_pages):
            keep = same_head & block_on(pg // ppb)
            m = jnp.maximum(m, jnp.max(jnp.where(keep, s_ref[pg], NEG_INF), axis=1, keepdims=True))
        p_own = jnp.exp(s_own - m)
        l = jnp.sum(p_own, axis=1, keepdims=True)
        for pg in range(n_pages):
            keep = same_head & block_on(pg // ppb)
            p = jnp.where(keep, jnp.exp(s_ref[pg] - m), 0.0)
            s_ref[pg] = p
            l = l + jnp.sum(p, axis=1, keepdims=True)
        inv = 1.0 / l
        po_ref[0] = p_own * inv
        for pg in range(n_pages):
            pp_ref[0, pg] = (s_ref[pg] * inv).astype(pp_ref.dtype)


def _moba_sample_apply_kernel(pt_ref, pp_ref, po_ref, vn_ref, *rest, t):
    del pt_ref
    pps = PAGES_PER_STEP
    v_refs = rest[:pps]
    o_ref, acc_ref = rest[pps:]
    c = pl.program_id(1)
    nc = pl.num_programs(1)

    @pl.when(c == 0)
    def _():
        acc_ref[...] = _mm(po_ref[0], vn_ref[...])

    acc = acc_ref[...]
    for pg in range(pps):
        acc = acc + _mm(pp_ref[0, pg], v_refs[pg][0])
    acc_ref[...] = acc

    @pl.when(c == nc - 1)
    def _():
        for h in range(HEADS):
            o_ref[:, h * DH:(h + 1) * DH] = acc_ref[h * t:(h + 1) * t, :]


def _moba_sample(q, k_new, v_new, cache_k, cache_v, page_table):
    bd, n_pages = page_table.shape
    n = q.shape[0]
    t = n // bd
    prow = cache_k.shape[1]
    page = prow // HEADS
    pps = PAGES_PER_STEP
    assert n_pages % pps == 0 and (n_pages * page) % MOBA_BLOCK == 0 and MOBA_BLOCK % page == 0
    assert SUBLANES % HEADS == 0 and pps % (MOBA_BLOCK // page) == 0
    nc = n_pages // pps
    ht = HEADS * t
    n_blk = n_pages * page // MOBA_BLOCK
    pt = page_table.reshape(-1).astype(I32)
    seq_q = pl.BlockSpec((t, HW), lambda b, c, pt: (b, 0))
    seq_kv = pl.BlockSpec((ht, DH), lambda b, c, pt: (b, 0))
    own = pl.BlockSpec((1, ht, ht), lambda b, c, pt: (b, 0, 0))

    def page_spec(i):
        return pl.BlockSpec((1, prow, DH), lambda b, c, pt: (pt[b * n_pages + c * pps + i], 0, 0))

    p_past, p_own = pl.pallas_call(
        functools.partial(_moba_sample_scores_kernel, t=t, n_pages=n_pages, page=page),
        grid_spec=pltpu.PrefetchScalarGridSpec(
            num_scalar_prefetch=1,
            grid=(bd, nc),
            in_specs=[seq_q, seq_kv] + [page_spec(i) for i in range(pps)],
            out_specs=[pl.BlockSpec((1, n_pages, ht, prow), lambda b, c, pt: (b, 0, 0, 0)), own],
            scratch_shapes=[pltpu.VMEM((n_pages, ht, prow), F32),
                            pltpu.VMEM((n_blk * SUBLANES, DH), F32)]),
        out_shape=[jax.ShapeDtypeStruct((bd, n_pages, ht, prow), BF16),
                   jax.ShapeDtypeStruct((bd, ht, ht), F32)],
        compiler_params=_params("parallel", "arbitrary"),
        name="moba_sample_scores",
    )(pt, q, k_new, *([cache_k] * pps))

    return pl.pallas_call(
        functools.partial(_moba_sample_apply_kernel, t=t),
        grid_spec=pltpu.PrefetchScalarGridSpec(
            num_scalar_prefetch=1,
            grid=(bd, nc),
            in_specs=[pl.BlockSpec((1, pps, ht, prow), lambda b, c, pt: (b, c, 0, 0)), own, seq_kv]
                     + [page_spec(i) for i in range(pps)],
            out_specs=seq_q,
            scratch_shapes=[pltpu.VMEM((ht, DH), F32)]),
        out_shape=jax.ShapeDtypeStruct((n, HW), F32),
        compiler_params=_params("parallel", "arbitrary"),
        name="moba_sample_apply",
    )(pt, p_past, p_own, v_new, *([cache_v] * pps))


def _postmix_kernel(dn_ref, mb_ref, x_ref, wo1_ref, wo2_ref, nx_ref, wq_ref, mk_ref, mv_ref, wxo_ref,
                    nf_ref, wr_ref, br_ref, tri_ref,
                    x2_ref, h2_ref, tope_ref, gate_ref, rank_ref, cnt_ref, carry_ref, *, seqs, n_exp):
    i = pl.program_id(0)

    @pl.when(i == 0)
    def _():
        carry_ref[...] = jnp.zeros_like(carry_ref)

    x1 = x_ref[...] + _mm(dn_ref[...], wo1_ref[...]) + _mm(mb_ref[...], wo2_ref[...])
    tm = x1.shape[0]
    q = _mm(_rms(x1, nx_ref[...]), wq_ref[...])
    scale = DH ** -0.5
    heads = []
    for h in range(HEADS):
        hs = slice(h * DH, (h + 1) * DH)
        if seqs == 1:
            s = _mm(q[:, hs], mk_ref[0, :, hs], NT_DIMS) * scale
            p = jnp.exp(s - jnp.max(s, axis=-1, keepdims=True))
            p = p / jnp.sum(p, axis=-1, keepdims=True)
            heads.append(_mm(p, mv_ref[0, :, hs]))
        else:
            q3 = q[:, hs].reshape(seqs, tm // seqs, DH).astype(BF16)
            s = jnp.einsum("gtd,gmd->gtm", q3, mk_ref[:, :, hs].astype(BF16), preferred_element_type=F32) * scale
            p = jnp.exp(s - jnp.max(s, axis=-1, keepdims=True))
            p = p / jnp.sum(p, axis=-1, keepdims=True)
            o = jnp.einsum("gtm,gmd->gtd", p.astype(BF16), mv_ref[:, :, hs].astype(BF16), preferred_element_type=F32)
            heads.append(o.reshape(tm, DH))
    x2 = x1 + _mm(jnp.concatenate(heads, axis=1), wxo_ref[...])
    x2_ref[...] = x2
    h2 = _rms(x2, nf_ref[...])
    h2_ref[...] = h2

    logits = _mm3(h2, wr_ref[...]) + br_ref[...]
    lt = logits.T[:n_exp]
    erow = lax.broadcasted_iota(I32, lt.shape, 0)
    picks = _top_picks(lt, TOP_K, erow, n_exp, 0)
    ex = [jnp.exp(val - picks[0][1]) for _, val in picks]
    den = ex[0]
    for e in ex[1:]:
        den = den + e
    onehot = [erow == idx for idx, _ in picks]
    any_hot = onehot[0]
    for oh in onehot[1:]:
        any_hot = any_hot | oh
    hot = any_hot.astype(F32)
    base = _mm(hot, tri_ref[...]) + carry_ref[:, 0:1]
    for k in range(TOP_K):
        tope_ref[k:k + 1, :] = picks[k][0]
        gate_ref[k:k + 1, :] = ex[k] / den
        rank_ref[k:k + 1, :] = jnp.sum(jnp.where(onehot[k], base, 0.0), axis=0, keepdims=True).astype(I32)
    pad = SUBLANES - TOP_K
    tope_ref[TOP_K:, :] = jnp.zeros((pad, tm), I32)
    gate_ref[TOP_K:, :] = jnp.zeros((pad, tm), F32)
    rank_ref[TOP_K:, :] = jnp.zeros((pad, tm), I32)
    carry = carry_ref[...] + jnp.sum(hot, axis=1, keepdims=True)
    carry_ref[...] = carry
    cnt_ref[...] = carry.astype(I32)


def _post_mix(dn, mb, x, mem_k, mem_v, w_out, norm_x, w_xq, w_xo, norm_ffn, w_router, b_router):
    n, d = x.shape
    n_seq, n_mem, _ = mem_k.shape
    rows_per_seq = n // n_seq
    n_exp = w_router.shape[1]
    if rows_per_seq >= ROW_TILE:
        tm, seqs = ROW_TILE, 1
        assert rows_per_seq % tm == 0
        mem_map = lambda i: (i * tm // rows_per_seq, 0, 0)
    else:
        seqs = min(max(SAMPLE_SEQS, LANES // rows_per_seq), n_seq)
        tm = seqs * rows_per_seq
        mem_map = lambda i: (i, 0, 0)
    wo = w_out.astype(BF16)
    wr = jnp.pad(w_router, ((0, 0), (0, LANES - n_exp)))
    br = jnp.pad(b_router, (0, LANES - n_exp)).reshape(1, LANES)
    tri = (jnp.arange(tm)[:, None] < jnp.arange(tm)[None, :]).astype(BF16)
    row = lambda w: pl.BlockSpec((tm, w), lambda i: (i, 0))
    col = pl.BlockSpec((SUBLANES, tm), lambda i: (0, i))
    mem_spec = pl.BlockSpec((seqs, n_mem, HW), mem_map)
    return pl.pallas_call(
        functools.partial(_postmix_kernel, seqs=seqs, n_exp=n_exp),
        grid=(n // tm,),
        in_specs=[row(HW), row(HW), row(d), _full((HW, d)), _full((HW, d)), _full((1, d)), _full((d, HW)),
                  mem_spec, mem_spec, _full((HW, d)), _full((1, d)), _full((d, LANES)), _full((1, LANES)),
                  _full((tm, tm))],
        out_specs=[row(d), row(d), col, col, col, _full((n_exp, LANES))],
        out_shape=[jax.ShapeDtypeStruct((n, d), F32), jax.ShapeDtypeStruct((n, d), F32),
                   jax.ShapeDtypeStruct((SUBLANES, n), I32), jax.ShapeDtypeStruct((SUBLANES, n), F32),
                   jax.ShapeDtypeStruct((SUBLANES, n), I32), jax.ShapeDtypeStruct((n_exp, LANES), I32)],
        scratch_shapes=[pltpu.VMEM((n_exp, LANES), F32)],
        compiler_params=_params("arbitrary"),
        name="post_mix",
    )(dn, mb, x, wo[:HW], wo[HW:], norm_x.reshape(1, d), w_xq.astype(BF16), mem_k, mem_v,
      w_xo.astype(BF16), norm_ffn.reshape(1, d), wr, br, tri)


def _plan_kernel(cnt_ref, tope_ref, rank_ref, dest_ref, be_ref, nused_ref, *, n_exp, bm, n_blocks):
    tope = tope_ref[...]
    dest = rank_ref[...]
    start = jnp.int32(0)
    first = pl.program_id(0) == 0
    for e in range(n_exp):
        dest = dest + jnp.where(tope == e, start * bm, 0)
        nb_e = (cnt_ref[e] + (bm - 1)) // bm

        @pl.when(first)
        def _(e=e, start=start, nb_e=nb_e):
            def fill(j, c):
                be_ref[start + j] = e
                return c
            lax.fori_loop(0, nb_e, fill, 0)

        start = start + nb_e
    dest_ref[...] = dest

    @pl.when(first)
    def _():
        nused_ref[0] = start

        def fill(j, c):
            be_ref[j] = n_exp - 1
            return c
        lax.fori_loop(start, n_blocks, fill, 0)


def _moe_plan(cnt, tope, rank, bm, n_blocks):
    n = tope.shape[1]
    n_exp = cnt.shape[0]
    tl = min(2048, n)
    col = pl.BlockSpec((SUBLANES, tl), lambda i: (0, i))
    smem = lambda: pl.BlockSpec(memory_space=pltpu.SMEM)
    return pl.pallas_call(
        functools.partial(_plan_kernel, n_exp=n_exp, bm=bm, n_blocks=n_blocks),
        grid=(n // tl,),
        in_specs=[smem(), col, col],
        out_specs=[col, smem(), smem()],
        out_shape=[jax.ShapeDtypeStruct((SUBLANES, n), I32), jax.ShapeDtypeStruct((n_blocks,), I32),
                   jax.ShapeDtypeStruct((1,), I32)],
        compiler_params=_params("arbitrary"),
        name="moe_plan",
    )(cnt[:, 0], tope, rank)


def _row_copy(src, src_row, dst, dst_row, sem):
    return pltpu.make_async_copy(src.at[pl.ds(src_row, 1), :], dst.at[pl.ds(dst_row, 1), :], sem)


def _dispatch_kernel(dest_ref, x_ref, xs_in_ref, xs_ref, sem, *, tm):
    del xs_in_ref

    def issue(t, c):
        for k in range(TOP_K):
            _row_copy(x_ref, t, xs_ref, dest_ref[k, t], sem).start()
        return c

    lax.fori_loop(0, tm, issue, 0)

    def drain(t, c):
        for k in range(TOP_K):
            _row_copy(x_ref, t, xs_ref, dest_ref[k, t], sem).wait()
        return c

    lax.fori_loop(0, tm, drain, 0)


def _moe_dispatch(h2, dest, n_slots):
    n, d = h2.shape
    tm = min(MOVE_TILE, n)
    return pl.pallas_call(
        functools.partial(_dispatch_kernel, tm=tm),
        grid=(n // tm,),
        in_specs=[pl.BlockSpec((SUBLANES, tm), lambda i: (0, i), memory_space=pltpu.SMEM),
                  pl.BlockSpec((tm, d), lambda i: (i, 0)),
                  pl.BlockSpec(memory_space=pl.ANY)],
        out_specs=pl.BlockSpec(memory_space=pl.ANY),
        out_shape=jax.ShapeDtypeStruct((n_slots, d), F32),
        scratch_shapes=[pltpu.SemaphoreType.DMA],
        input_output_aliases={2: 0},
        compiler_params=_params("arbitrary"),
        name="moe_dispatch",
    )(dest, h2, jnp.zeros((n_slots, d), F32))


def _expert_kernel(be_ref, nused_ref, x_ref, wg_ref, bg_ref, wu_ref, bu_ref, wd_ref, bd_ref, y_ref):
    del be_ref
    i = pl.program_id(0)

    @pl.when(i < nused_ref[0])
    def _():
        x = x_ref[...].astype(BF16)
        g = jnp.minimum(_mm(x, wg_ref[0]) + bg_ref[0], SWIGLU_LIMIT)
        u = jnp.clip(_mm(x, wu_ref[0]) + bu_ref[0], -SWIGLU_LIMIT, SWIGLU_LIMIT)
        a = g * _sigmoid(SWIGLU_ALPHA * g) * (u + 1.0)
        y_ref[...] = _mm(a, wd_ref[0]) + bd_ref[0]

    @pl.when(i >= nused_ref[0])
    def _():
        y_ref[...] = jnp.zeros_like(y_ref)


def _moe_experts(xs, block_e, n_used, w_gate, b_gate, w_up, b_up, w_down, b_down, bm):
    n_slots, d = xs.shape
    n_exp, _, d_ff = w_gate.shape
    n_blocks = n_slots // bm
    wspec = lambda r, c: pl.BlockSpec((1, r, c), lambda i, be, nu: (be[i], 0, 0))
    return pl.pallas_call(
        _expert_kernel,
        grid_spec=pltpu.PrefetchScalarGridSpec(
            num_scalar_prefetch=2,
            grid=(n_blocks,),
            in_specs=[pl.BlockSpec((bm, d), lambda i, be, nu: (i, 0)),
                      wspec(d, d_ff), wspec(1, d_ff), wspec(d, d_ff), wspec(1, d_ff), wspec(d_ff, d), wspec(1, d)],
            out_specs=pl.BlockSpec((bm, d), lambda i, be, nu: (i, 0))),
        out_shape=jax.ShapeDtypeStruct((n_slots, d), F32),
        compiler_params=_params("arbitrary"),
        name="moe_experts",
    )(block_e, n_used, xs, w_gate.astype(BF16), b_gate.reshape(n_exp, 1, d_ff), w_up.astype(BF16),
      b_up.reshape(n_exp, 1, d_ff), w_down.astype(BF16), b_down.reshape(n_exp, 1, d))


def _combine_kernel(dest_ref, gate_ref, x_ref, nw_ref, ys_ref, o_ref, buf_ref, sem, *, tm):
    def issue(t, c):
        for k in range(TOP_K):
            _row_copy(ys_ref, dest_ref[k, t], buf_ref.at[k], t, sem).start()
        return c

    lax.fori_loop(0, tm, issue, 0)

    def drain(t, c):
        for k in range(TOP_K):
            _row_copy(ys_ref, dest_ref[k, t], buf_ref.at[k], t, sem).wait()
        return c

    lax.fori_loop(0, tm, drain, 0)
    gt = gate_ref[...].T
    acc = x_ref[...]
    for k in range(TOP_K):
        acc = acc + gt[:, k:k + 1] * buf_ref[k]
    o_ref[...] = _rms(acc, nw_ref[...])


def _moe_combine(ys, dest, gates, x2, norm_final):
    n, d = x2.shape
    tm = min(MOVE_TILE, n)
    col = lambda ms: pl.BlockSpec((SUBLANES, tm), lambda i: (0, i), memory_space=ms)
    return pl.pallas_call(
        functools.partial(_combine_kernel, tm=tm),
        grid=(n // tm,),
        in_specs=[col(pltpu.SMEM), col(pltpu.VMEM), pl.BlockSpec((tm, d), lambda i: (i, 0)), _full((1, d)),
                  pl.BlockSpec(memory_space=pl.ANY)],
        out_specs=pl.BlockSpec((tm, d), lambda i: (i, 0)),
        out_shape=jax.ShapeDtypeStruct((n, d), F32),
        scratch_shapes=[pltpu.VMEM((TOP_K, tm, d), F32), pltpu.SemaphoreType.DMA],
        compiler_params=_params("arbitrary"),
        name="moe_combine",
    )(dest, gates, x2, norm_final.reshape(1, d), ys)


def _moe_and_final_norm(x2, h2, tope, gates, rank, cnt, w_gate, b_gate, w_up, b_up, w_down, b_down, norm_final):
    n = x2.shape[0]
    n_exp = w_gate.shape[0]
    bm = MOE_BLOCK_ROWS
    n_blocks = -(-(n * TOP_K + n_exp * (bm - 1)) // bm)
    dest, block_e, n_used = _moe_plan(cnt, tope, rank, bm, n_blocks)
    xs = _moe_dispatch(h2, dest, n_blocks * bm)
    ys = _moe_experts(xs, block_e, n_used, w_gate, b_gate, w_up, b_up, w_down, b_down, bm)
    return _moe_combine(ys, dest, gates, x2, norm_final)


def kernel(x_prompt, x_sample, mem_prompt, cache_k, cache_v, page_table, state_delta, state_conv, cache_mem_k, cache_mem_v, norm_mix, w_in, conv_w, a_log, dt_bias, dn_norm, w_out, norm_x, norm_mem, w_xq, w_xk, w_xv, w_xo, norm_ffn, w_router, b_router, w_gate, b_gate, w_up, b_up, w_down, b_down, norm_final):
    depth = w_in.shape[0]
    assert depth == 1, "single-layer stack"
    b, s, d = x_prompt.shape
    bd, t, _ = x_sample.shape
    n_mem = mem_prompt.shape[1]
    n_pool, page = cache_k.shape[1], cache_k.shape[2]
    past = page_table.shape[1] * page
    assert past % MOBA_BLOCK == 0, "past length must be whole MoBA blocks"
    l = 0
    moe_w = (w_gate[l], b_gate[l], w_up[l], b_up[l], w_down[l], b_down[l])

    xs = x_sample.reshape(bd * t, d)
    tile_s = min(ROW_TILE, bd * t)
    cos_s, sin_s = _rope_tables(past + jnp.arange(tile_s) % t)
    conv_s, z_s, gb_s, q_s, k_s, v_s = _in_proj(xs, norm_mix[l], w_in[l], a_log[l], dt_bias[l],
                                                cos_s, sin_s, prompt=False)
    dn_s, s_s, cv_s = _deltanet_sample(conv_s, z_s, gb_s, state_conv[l], state_delta[l], conv_w[l], dn_norm[l])
    k4_s = k_s.reshape(bd * t * HEADS, DH)
    v4_s = v_s.reshape(bd * t * HEADS, DH)
    mb_s = _moba_sample(q_s, k4_s, v4_s, cache_k.reshape(n_pool, page * HEADS, DH),
                        cache_v.reshape(n_pool, page * HEADS, DH), page_table)
    x2s, h2s, tope_s, gates_s, rank_s, cnt_s = _post_mix(
        dn_s, mb_s, xs, cache_mem_k.reshape(bd, n_mem, HW), cache_mem_v.reshape(bd, n_mem, HW),
        w_out[l], norm_x[l], w_xq[l], w_xo[l], norm_ffn[l], w_router[l], b_router[l])
    y_sample = _moe_and_final_norm(x2s, h2s, tope_s, gates_s, rank_s, cnt_s, *moe_w, norm_final)

    xp = x_prompt.reshape(b * s, d)
    cos_p, sin_p = _rope_tables(jnp.arange(s))
    conv_in, z, gb, q, k, v, kb, vt, kmean = _in_proj(xp, norm_mix[l], w_in[l], a_log[l], dt_bias[l],
                                                      cos_p, sin_p, prompt=True)
    dn, s_p, cv_p = _deltanet_prompt(conv_in, z, gb, conv_w[l], dn_norm[l], b)
    mb = _moba_prompt(q, kb, vt, kmean, b)
    mk_p, mv_p = _mem_kv(mem_prompt.reshape(b * n_mem, d), norm_mem[l], w_xk[l], w_xv[l])
    x2, h2, tope, gates, rank, cnt = _post_mix(
        dn, mb, xp, mk_p.reshape(b, n_mem, HW), mv_p.reshape(b, n_mem, HW), w_out[l], norm_x[l], w_xq[l],
        w_xo[l], norm_ffn[l], w_router[l], b_router[l])
    y_prompt = _moe_and_final_norm(x2, h2, tope, gates, rank, cnt, *moe_w, norm_final)

    hd = (HEADS, DH)
    return (y_prompt.reshape(b, s, d), y_sample.reshape(bd, t, d),
            k.reshape(1, b, s, *hd), v.reshape(1, b, s, *hd),
            k4_s.reshape(1, bd, t, *hd), v4_s.reshape(1, bd, t, *hd),
            s_p[None], s_s[None], cv_p[None], cv_s[None],
            mk_p.reshape(1, b, n_mem, *hd), mv_p.reshape(1, b, n_mem, *hd))
```

```python
import functools
import math

import jax
import jax.numpy as jnp
from jax import lax
from jax.experimental import pallas as pl
from jax.experimental.pallas import tpu as pltpu

F32 = jnp.float32
BF16 = jnp.bfloat16
I32 = jnp.int32

HEADS = 4
DH = 128
HW = HEADS * DH
CONV_W = 4
DN_CHUNK = 64
MOBA_BLOCK = 256
MOBA_TOPK = 3
ROPE_THETA = 10000.0
TOP_K = 4
SWIGLU_ALPHA = 1.702
SWIGLU_LIMIT = 7.0
EPS = 1e-6
NEG_INF = float("-inf")
LOG2_E = math.log2(math.e)

LANES = 128
SUBLANES = 8
VMEM_LIMIT = 56 * 1024 * 1024

ROW_TILE = 512
DN_GROUP = 256
MOBA_GROUP = 4
MOE_BLOCK_ROWS = 512
MOVE_TILE = 256
PAGES_PER_STEP = 16
SAMPLE_SEQS = 8

NN_DIMS = (((1,), (0,)), ((), ()))
NT_DIMS = (((1,), (1,)), ((), ()))
TN_DIMS = (((0,), (0,)), ((), ()))


def _mm(a, b, dims=NN_DIMS):
    return lax.dot_general(a.astype(BF16), b.astype(BF16), dims, preferred_element_type=F32)


def _split(a):
    hi = a.astype(BF16)
    lo = (a - hi.astype(F32)).astype(BF16)
    return hi, lo


def _mm3(a, b, dims=NN_DIMS):
    a_hi, a_lo = _split(a)
    b_hi, b_lo = _split(b)
    d = functools.partial(lax.dot_general, dimension_numbers=dims, preferred_element_type=F32)
    return d(a_hi, b_hi) + (d(a_hi, b_lo) + d(a_lo, b_hi))


def _rms(x, w):
    return x * lax.rsqrt(jnp.mean(x * x, axis=-1, keepdims=True) + EPS) * w


def _sigmoid(x):
    return 1.0 / (1.0 + jnp.exp(-x))


def _softplus(x):
    return jnp.maximum(x, 0.0) + jnp.log1p(jnp.exp(-jnp.abs(x)))


def _params(*sem):
    return pltpu.CompilerParams(dimension_semantics=sem, vmem_limit_bytes=VMEM_LIMIT)


def _full(shape):
    nd = len(shape)
    return pl.BlockSpec(shape, lambda *_: (0,) * nd)


def _top_picks(scores, n_pick, ids, n_ids, axis):
    picks = []
    cur = scores
    for _ in range(n_pick):
        m = jnp.max(cur, axis=axis, keepdims=True)
        idx = jnp.min(jnp.where(cur == m, ids, n_ids), axis=axis, keepdims=True)
        picks.append((idx, m))
        cur = jnp.where(ids == idx, NEG_INF, cur)
    return picks


def _memkv_kernel(m_ref, nw_ref, wk_ref, wv_ref, k_ref, v_ref):
    h = _rms(m_ref[...], nw_ref[...])
    k_ref[...] = _mm(h, wk_ref[...])
    v_ref[...] = _mm(h, wv_ref[...])


def _mem_kv(mem, norm_mem, w_xk, w_xv):
    n, d = mem.shape
    tm = min(ROW_TILE, n)
    return pl.pallas_call(
        _memkv_kernel,
        grid=(n // tm,),
        in_specs=[pl.BlockSpec((tm, d), lambda i: (i, 0)), _full((1, d)), _full(w_xk.shape), _full(w_xv.shape)],
        out_specs=[pl.BlockSpec((tm, HW), lambda i: (i, 0))] * 2,
        out_shape=[jax.ShapeDtypeStruct((n, HW), F32)] * 2,
        compiler_params=_params("parallel"),
        name="mem_kv",
    )(mem, norm_mem.reshape(1, d), w_xk.astype(BF16), w_xv.astype(BF16))


def _rope(x, cos, sin):
    return x * cos + pltpu.roll(x, DH // 2, axis=1) * sin


def _inproj_kernel(x_ref, nw_ref, w_ref, wab_ref, cos_ref, sin_ref, alog_ref, dtb_ref,
                   conv_ref, z_ref, gb_ref, q_ref, k_ref, v_ref, *rest, conv_ch, prompt):
    x = x_ref[...]
    hb = _rms(x, nw_ref[...]).astype(BF16)
    conv_ref[...] = _mm(hb, w_ref[:, 0:conv_ch])
    z_ref[...] = _mm(hb, w_ref[:, conv_ch:conv_ch + HW])
    ab = _mm(hb, wab_ref[...])
    lane = lax.broadcasted_iota(I32, ab.shape, 1)
    g = -jnp.exp(alog_ref[...]) * _softplus(ab + dtb_ref[...])
    gb_ref[...] = jnp.where(lane < HEADS, g, _sigmoid(ab))
    off = conv_ch + HW
    cos = cos_ref[...]
    sin = sin_ref[...]
    q = _mm(hb, w_ref[:, off:off + HW])
    k = _mm(hb, w_ref[:, off + HW:off + 2 * HW])
    v = _mm(hb, w_ref[:, off + 2 * HW:off + 3 * HW])
    k_rot = []
    for h in range(HEADS):
        sl = slice(h * DH, (h + 1) * DH)
        q_ref[:, sl] = _rope(q[:, sl], cos, sin)
        k_rot.append(_rope(k[:, sl], cos, sin))
        k_ref[:, sl] = k_rot[h]
    v_ref[...] = v
    if prompt:
        kb_ref, vt_ref, kmean_ref = rest
        for h in range(HEADS):
            kb_ref[:, h * DH:(h + 1) * DH] = k_rot[h].astype(BF16)
        for blk in range(x.shape[0] // MOBA_BLOCK):
            rows = slice(blk * MOBA_BLOCK, (blk + 1) * MOBA_BLOCK)
            vt_ref[blk] = v[rows].T.astype(BF16)
            for h in range(HEADS):
                kmean_ref[blk, :, h * DH:(h + 1) * DH] = jnp.mean(k_rot[h][rows], axis=0, keepdims=True)


def _in_proj(x, norm_w, w_in, a_log, dt_bias, cos, sin, *, prompt):
    n, d = x.shape
    conv_ch = 3 * HW
    tm = min(ROW_TILE, n)
    off_a = conv_ch + HW
    w_main = jnp.concatenate([w_in[:, :off_a], w_in[:, off_a + 2 * HEADS:]], axis=1).astype(BF16)
    w_ab = jnp.pad(w_in[:, off_a:off_a + 2 * HEADS], ((0, 0), (0, LANES - 2 * HEADS))).astype(BF16)
    alog = jnp.pad(a_log, (0, LANES - HEADS)).reshape(1, LANES)
    dtb = jnp.pad(dt_bias, (0, LANES - HEADS)).reshape(1, LANES)
    n_pos = cos.shape[0] // tm
    row = lambda w: pl.BlockSpec((tm, w), lambda i: (i, 0))
    out_specs = [row(conv_ch), row(HW), row(LANES), row(HW), row(HW), row(HW)]
    out_shape = [jax.ShapeDtypeStruct((n, w), F32) for w in (conv_ch, HW, LANES, HW, HW, HW)]
    if prompt:
        nb = tm // MOBA_BLOCK
        out_specs += [row(HW),
                      pl.BlockSpec((nb, HW, MOBA_BLOCK), lambda i: (i, 0, 0)),
                      pl.BlockSpec((nb, 1, HW), lambda i: (i, 0, 0))]
        out_shape += [jax.ShapeDtypeStruct((n, HW), BF16),
                      jax.ShapeDtypeStruct((n // MOBA_BLOCK, HW, MOBA_BLOCK), BF16),
                      jax.ShapeDtypeStruct((n // MOBA_BLOCK, 1, HW), F32)]
    return pl.pallas_call(
        functools.partial(_inproj_kernel, conv_ch=conv_ch, prompt=prompt),
        grid=(n // tm,),
        in_specs=[row(d), _full((1, d)), _full(w_main.shape), _full(w_ab.shape),
                  pl.BlockSpec((tm, DH), lambda i: (i % n_pos, 0)),
                  pl.BlockSpec((tm, DH), lambda i: (i % n_pos, 0)),
                  _full((1, LANES)), _full((1, LANES))],
        out_specs=out_specs,
        out_shape=out_shape,
        compiler_params=_params("parallel"),
        name="in_proj",
    )(x, norm_w.reshape(1, d), w_main, w_ab, cos, sin, alog, dtb)


def _rope_tables(pos):
    half = DH // 2
    inv = ROPE_THETA ** (-jnp.arange(half, dtype=F32) / half)
    ang = pos.astype(F32)[:, None] * inv[None, :]
    c, s = jnp.cos(ang), jnp.sin(ang)
    return jnp.concatenate([c, c], axis=1), jnp.concatenate([-s, s], axis=1)


def _chunk_cumsum(g, row_in_chunk):
    for sh in (1, 2, 4, 8, 16, 32):
        g = g + jnp.where(row_in_chunk >= sh, pltpu.roll(g, sh, axis=0), 0.0)
    return g


def _unit_lower_inverse(a):
    n = a.shape[0]
    eye = (lax.broadcasted_iota(I32, (n, n), 0) == lax.broadcasted_iota(I32, (n, n), 1)).astype(F32)
    p = eye - a
    ak = a
    for _ in range(5):
        ak = _mm(ak, ak)
        p = p + _mm(p, ak)
    return p


def _gated_out(o, z, dn_w):
    return _rms(o, dn_w) * (z * _sigmoid(z))


def _dn_prompt_kernel(u_ref, halo_ref, z_ref, gb_ref, cw_ref, dnw_ref,
                      o_ref, sfin_ref, cnew_ref,
                      ext_ref, s_ref, wq_ref, us_ref, kd_ref, vn_ref, oi_ref, qk_ref, gl_ref, *, tm):
    i = pl.program_id(1)
    nt = pl.num_programs(1)
    nc = tm // DN_CHUNK
    cpg = DN_GROUP // DN_CHUNK
    hal = CONV_W - 1

    @pl.when(i == 0)
    def _():
        s_ref[...] = jnp.zeros_like(s_ref)

    u = u_ref[...]
    ext_ref[0:SUBLANES, :] = jnp.where(i > 0, halo_ref[...], 0.0)
    ext_ref[SUBLANES:SUBLANES + tm, :] = u
    cw = cw_ref[...]
    y = u * cw[hal:hal + 1]
    for j in range(hal):
        y = y + ext_ref[SUBLANES - hal + j:SUBLANES - hal + j + tm, :] * cw[j:j + 1]
    qkv = y * _sigmoid(y)

    gb = gb_ref[...]
    row_in_chunk = lax.broadcasted_iota(I32, gb.shape, 0) % DN_CHUNK
    gc = _chunk_cumsum(gb, row_in_chunk)
    gc3 = gc.reshape(nc, DN_CHUNK, LANES)
    glast = jnp.broadcast_to(gc3[:, DN_CHUNK - 1:DN_CHUNK, :], gc3.shape).reshape(tm, LANES)
    gl_ref[...] = jnp.exp(glast)
    gct = gc.T

    gi = lax.broadcasted_iota(I32, (DN_GROUP, DN_GROUP), 0)
    gj = lax.broadcasted_iota(I32, (DN_GROUP, DN_GROUP), 1)
    same = (gi // DN_CHUNK) == (gj // DN_CHUNK)
    causal = same & (gi >= gj)
    strict = same & (gi > gj)

    for h in range(HEADS):
        qh = qkv[:, h * DH:(h + 1) * DH]
        kh = qkv[:, HW + h * DH:HW + (h + 1) * DH]
        vh = qkv[:, 2 * HW + h * DH:2 * HW + (h + 1) * DH]
        qn = qh * lax.rsqrt(jnp.sum(qh * qh, axis=-1, keepdims=True) + EPS) * (DH ** -0.5)
        kn = kh * lax.rsqrt(jnp.sum(kh * kh, axis=-1, keepdims=True) + EPS)
        gcol = gc[:, h:h + 1]
        beta = gb[:, HEADS + h:HEADS + h + 1]
        eg = jnp.exp(gcol)
        kbeta = kn * beta
        kd_ref[h] = kn * jnp.exp(glast[:, h:h + 1] - gcol)
        qe = qn * eg
        for r in range(tm // DN_GROUP):
            rs = slice(r * DN_GROUP, (r + 1) * DN_GROUP)
            decay = jnp.where(causal, jnp.exp(jnp.minimum(gcol[rs] - gct[h:h + 1, rs], 0.0)), 0.0)
            a = jnp.where(strict, _mm(kbeta[rs], kn[rs], NT_DIMS) * decay, 0.0)
            qk_ref[h, rs, :] = _mm(qn[rs], kn[rs], NT_DIMS) * decay
            t_inv = _unit_lower_inverse(a)
            rhs = jnp.concatenate([vh[rs] * beta[rs], kbeta[rs] * eg[rs]], axis=1)
            uw = _mm(t_inv, rhs)
            us_ref[h, rs, :] = uw[:, :DH]
            for c in range(cpg):
                lo = r * DN_GROUP + c * DN_CHUNK
                wq_ref[h, r * cpg + c, 0:DN_CHUNK, :] = uw[c * DN_CHUNK:(c + 1) * DN_CHUNK, DH:]
                wq_ref[h, r * cpg + c, DN_CHUNK:2 * DN_CHUNK, :] = qe[lo:lo + DN_CHUNK]

    def chunk_step(c, carry):
        r0 = pl.multiple_of(c * DN_CHUNK, DN_CHUNK)
        rows = pl.ds(r0, DN_CHUNK)
        for h in range(HEADS):
            s = s_ref[h]
            m1 = _mm(wq_ref[h, c], s)
            v_new = us_ref[h, rows, :] - m1[:DN_CHUNK]
            oi_ref[h, rows, :] = m1[DN_CHUNK:]
            vn_ref[h, rows, :] = v_new
            s_ref[h] = s * gl_ref[pl.ds(r0, 1), h:h + 1] + _mm(kd_ref[h, rows, :], v_new, TN_DIMS)
        return carry

    lax.fori_loop(0, nc, chunk_step, 0)

    z = z_ref[...]
    dnw = dnw_ref[...]
    for h in range(HEADS):
        parts = []
        for r in range(tm // DN_GROUP):
            rs = slice(r * DN_GROUP, (r + 1) * DN_GROUP)
            parts.append(oi_ref[h, rs, :] + _mm(qk_ref[h, rs, :], vn_ref[h, rs, :]))
        o = jnp.concatenate(parts, axis=0)
        o_ref[:, h * DH:(h + 1) * DH] = _gated_out(o, z[:, h * DH:(h + 1) * DH], dnw).astype(o_ref.dtype)

    @pl.when(i == nt - 1)
    def _():
        sfin_ref[0] = s_ref[...]
        cnew_ref[0] = ext_ref[SUBLANES + tm - hal:SUBLANES + tm, :]


def _deltanet_prompt(conv_in, z, gb, conv_w, dn_norm, batch):
    n, conv_ch = conv_in.shape
    s = n // batch
    tm = min(ROW_TILE, s)
    nt = s // tm
    hal = CONV_W - 1
    hb = tm // SUBLANES
    row = lambda w: pl.BlockSpec((tm, w), lambda b, i: (b * nt + i, 0))
    head_scr = lambda w: pltpu.VMEM((HEADS, tm, w), F32)
    return pl.pallas_call(
        functools.partial(_dn_prompt_kernel, tm=tm),
        grid=(batch, nt),
        in_specs=[row(conv_ch),
                  pl.BlockSpec((SUBLANES, conv_ch), lambda b, i: (jnp.maximum((b * nt + i) * hb - 1, 0), 0)),
                  row(HW), row(LANES), _full((CONV_W, conv_ch)), _full((1, DH))],
        out_specs=[row(HW),
                   pl.BlockSpec((1, HEADS, DH, DH), lambda b, i: (b, 0, 0, 0)),
                   pl.BlockSpec((1, hal, conv_ch), lambda b, i: (b, 0, 0))],
        out_shape=[jax.ShapeDtypeStruct((n, HW), BF16),
                   jax.ShapeDtypeStruct((batch, HEADS, DH, DH), F32),
                   jax.ShapeDtypeStruct((batch, hal, conv_ch), F32)],
        scratch_shapes=[pltpu.VMEM((SUBLANES + tm, conv_ch), F32),
                        pltpu.VMEM((HEADS, DH, DH), F32),
                        pltpu.VMEM((HEADS, tm // DN_CHUNK, 2 * DN_CHUNK, DH), F32),
                        head_scr(DH), head_scr(DH), head_scr(DH), head_scr(DH), head_scr(DN_GROUP),
                        pltpu.VMEM((tm, LANES), F32)],
        compiler_params=_params("arbitrary", "arbitrary"),
        name="deltanet_prompt",
    )(conv_in, conv_in, z, gb, conv_w, dn_norm.reshape(1, DH))


def _dn_sample_kernel(u_ref, cbuf_ref, z_ref, gb_ref, st_ref, cw_ref, dnw_ref,
                      o_ref, snew_ref, cnew_ref, ext_ref, *, t, seqs):
    hal = CONV_W - 1
    cw = cw_ref[...]
    dnw = dnw_ref[...]

    def seq_step(si, carry):
        r0 = pl.multiple_of(si * t, t)
        rows = pl.ds(r0, t)
        ext_ref[SUBLANES - hal:SUBLANES, :] = cbuf_ref[si]
        ext_ref[SUBLANES:SUBLANES + t, :] = u_ref[rows, :]
        y = ext_ref[SUBLANES - hal:SUBLANES - hal + t, :] * cw[0:1]
        for j in range(1, CONV_W):
            y = y + ext_ref[SUBLANES - hal + j:SUBLANES - hal + j + t, :] * cw[j:j + 1]
        cnew_ref[si] = ext_ref[SUBLANES + t - hal:SUBLANES + t, :]
        qkv = y * _sigmoid(y)
        gb = gb_ref[rows, :]
        z = z_ref[rows, :]
        for h in range(HEADS):
            qh = qkv[:, h * DH:(h + 1) * DH]
            kh = qkv[:, HW + h * DH:HW + (h + 1) * DH]
            vh = qkv[:, 2 * HW + h * DH:2 * HW + (h + 1) * DH]
            qn = qh * lax.rsqrt(jnp.sum(qh * qh, axis=-1, keepdims=True) + EPS) * (DH ** -0.5)
            kn = kh * lax.rsqrt(jnp.sum(kh * kh, axis=-1, keepdims=True) + EPS)
            qt = qn.T
            kt = kn.T
            eg = jnp.exp(gb[:, h:h + 1])
            beta = gb[:, HEADS + h:HEADS + h + 1]
            s = st_ref[si, h]
            outs = []
            for ti in range(t):
                kcol = kt[:, ti:ti + 1]
                s = s * eg[ti:ti + 1]
                v_new = beta[ti:ti + 1] * (vh[ti:ti + 1] - jnp.sum(s * kcol, axis=0, keepdims=True))
                s = s + kcol * v_new
                outs.append(jnp.sum(s * qt[:, ti:ti + 1], axis=0, keepdims=True))
            snew_ref[si, h] = s
            o = jnp.concatenate(outs, axis=0)
            o_ref[rows, h * DH:(h + 1) * DH] = _gated_out(o, z[:, h * DH:(h + 1) * DH], dnw)
        return carry

    lax.fori_loop(0, seqs, seq_step, 0)


def _deltanet_sample(conv_in, z, gb, state_conv, state_delta, conv_w, dn_norm):
    n, conv_ch = conv_in.shape
    bd = state_delta.shape[0]
    t = n // bd
    seqs = min(SAMPLE_SEQS, bd)
    hal = CONV_W - 1
    row = lambda w: pl.BlockSpec((seqs * t, w), lambda i: (i, 0))
    st_spec = pl.BlockSpec((seqs, HEADS, DH, DH), lambda i: (i, 0, 0, 0))
    cb_spec = pl.BlockSpec((seqs, hal, conv_ch), lambda i: (i, 0, 0))
    return pl.pallas_call(
        functools.partial(_dn_sample_kernel, t=t, seqs=seqs),
        grid=(bd // seqs,),
        in_specs=[row(conv_ch), cb_spec, row(HW), row(LANES), st_spec, _full((CONV_W, conv_ch)), _full((1, DH))],
        out_specs=[row(HW), st_spec, cb_spec],
        out_shape=[jax.ShapeDtypeStruct((n, HW), F32),
                   jax.ShapeDtypeStruct(state_delta.shape, F32),
                   jax.ShapeDtypeStruct(state_conv.shape, F32)],
        scratch_shapes=[pltpu.VMEM((SUBLANES + t, conv_ch), F32)],
        compiler_params=_params("parallel"),
        name="deltanet_sample",
    )(conv_in, state_conv, z, gb, state_delta, conv_w, dn_norm.reshape(1, DH))


def _moba_prompt_kernel(q_ref, k_ref, vt_ref, km_ref, o_ref, sel_ref, m_ref, l_ref, acc_ref,
                        sa_ref, sb_ref, pa_ref, pb_ref, *, nb):
    qb = pl.program_id(2)
    blk = MOBA_BLOCK
    grp = MOBA_GROUP
    q = q_ref[...]
    qs = (q * (DH ** -0.5 * LOG2_E)).astype(BF16)

    gs = _mm3(km_ref[0], q, NT_DIMS)
    brow = lax.broadcasted_iota(I32, gs.shape, 0)
    gs = jnp.where(brow < qb, gs, NEG_INF)
    sel = jnp.zeros(gs.shape, F32)
    for idx, val in _top_picks(gs, MOBA_TOPK, brow, nb, 0):
        sel = jnp.where((brow == idx) & (val > NEG_INF), 1.0, sel)
    sel_ref[...] = sel

    def scores(row0, rows):
        kj = k_ref[pl.ds(pl.multiple_of(row0, blk), rows), :]
        return lax.dot_general(kj, qs, NT_DIMS, preferred_element_type=F32)

    ki = lax.broadcasted_iota(I32, (blk, blk), 0)
    qi = lax.broadcasted_iota(I32, (blk, blk), 1)
    s = jnp.where(ki <= qi, scores(qb * blk, blk), NEG_INF)
    m = jnp.max(s, axis=0, keepdims=True)
    p = jnp.exp2(s - m)
    m_ref[...] = m
    l_ref[...] = jnp.sum(p, axis=0, keepdims=True)
    acc_ref[...] = _mm(vt_ref[qb], p)

    n_groups = (qb + grp - 1) // grp
    last_group = nb // grp - 1
    sa_ref[...] = scores(0, grp * blk)
    pb_ref[...] = jnp.zeros_like(pb_ref)

    def apply_probs(g, p_ref):
        acc = acc_ref[...]
        for b in range(grp):
            acc = acc + lax.dot_general(vt_ref[g * grp + b], p_ref[b], NN_DIMS, preferred_element_type=F32)
        return acc

    def step(g, s_cur, s_next, p_prev, p_cur):
        s_next[...] = scores(jnp.minimum(g + 1, last_group) * (grp * blk), grp * blk)
        acc = apply_probs(jnp.maximum(g - 1, 0), p_prev)
        j0 = g * grp
        m_old = m_ref[...]
        m_new = m_old
        picked = []
        for b in range(grp):
            on = sel_ref[pl.ds(j0 + b, 1), :] > 0.5
            col_max = jnp.max(s_cur[b * blk:(b + 1) * blk, :], axis=0, keepdims=True)
            m_new = jnp.maximum(m_new, jnp.where(on, col_max, NEG_INF))
            picked.append(on)
        alpha = jnp.exp2(m_old - m_new)
        l = alpha * l_ref[...]
        for b in range(grp):
            p = jnp.exp2(s_cur[b * blk:(b + 1) * blk, :] - jnp.where(picked[b], m_new, -NEG_INF))
            l = l + jnp.sum(p, axis=0, keepdims=True)
            p_cur[b] = p.astype(BF16)
        m_ref[...] = m_new
        l_ref[...] = l
        acc_ref[...] = alpha * acc

    def group_pair(i, carry):
        step(2 * i, sa_ref, sb_ref, pb_ref, pa_ref)
        step(2 * i + 1, sb_ref, sa_ref, pa_ref, pb_ref)
        return carry

    lax.fori_loop(0, n_groups // 2, group_pair, 0)
    odd = n_groups % 2 == 1

    @pl.when(odd)
    def _():
        step(n_groups - 1, sa_ref, sb_ref, pb_ref, pa_ref)
        o_ref[...] = (apply_probs(n_groups - 1, pa_ref) / l_ref[...]).T.astype(o_ref.dtype)

    @pl.when(jnp.logical_not(odd))
    def _():
        o_ref[...] = (apply_probs(jnp.maximum(n_groups - 1, 0), pb_ref) / l_ref[...]).T.astype(o_ref.dtype)


def _moba_prompt(q, kb, vt, kmean, batch):
    n = q.shape[0]
    s = n // batch
    nb = s // MOBA_BLOCK
    assert nb % MOBA_GROUP == 0
    return pl.pallas_call(
        functools.partial(_moba_prompt_kernel, nb=nb),
        grid=(batch, HEADS, nb),
        in_specs=[pl.BlockSpec((MOBA_BLOCK, DH), lambda b, h, j: (b * nb + j, h)),
                  pl.BlockSpec((s, DH), lambda b, h, j: (b, h)),
                  pl.BlockSpec((nb, DH, MOBA_BLOCK), lambda b, h, j: (b, h, 0)),
                  pl.BlockSpec((1, nb, DH), lambda b, h, j: (b, 0, h))],
        out_specs=pl.BlockSpec((MOBA_BLOCK, DH), lambda b, h, j: (b * nb + j, h)),
        out_shape=jax.ShapeDtypeStruct((n, HW), BF16),
        scratch_shapes=[pltpu.VMEM((nb, MOBA_BLOCK), F32), pltpu.VMEM((1, MOBA_BLOCK), F32),
                        pltpu.VMEM((1, MOBA_BLOCK), F32), pltpu.VMEM((DH, MOBA_BLOCK), F32),
                        pltpu.VMEM((MOBA_GROUP * MOBA_BLOCK, MOBA_BLOCK), F32),
                        pltpu.VMEM((MOBA_GROUP * MOBA_BLOCK, MOBA_BLOCK), F32),
                        pltpu.VMEM((MOBA_GROUP, MOBA_BLOCK, MOBA_BLOCK), BF16),
                        pltpu.VMEM((MOBA_GROUP, MOBA_BLOCK, MOBA_BLOCK), BF16)],
        compiler_params=_params("parallel", "parallel", "arbitrary"),
        name="moba_prompt",
    )(q, kb, vt, kmean.reshape(batch, nb, HW))


def _moba_sample_scores_kernel(pt_ref, q_ref, kn_ref, *rest, t, n_pages, page, pps):
    del pt_ref
    k_refs = rest[:pps]
    pp_ref, po_ref, s_ref, ksum_ref = rest[pps:]
    c = pl.program_id(1)
    nc = pl.num_programs(1)
    ppb = MOBA_BLOCK // page
    n_blk = n_pages // ppb
    fold = SUBLANES // HEADS
    scale = DH ** -0.5
    q = q_ref[...]
    qall = jnp.concatenate([q[:, h * DH:(h + 1) * DH] for h in range(HEADS)], axis=0)
    qbf = (qall * scale).astype(BF16)

    for jb in range(pps // ppb):
        ksum = None
        for jp in range(ppb):
            pg = jb * ppb + jp
            kp = k_refs[pg][0]
            s_ref[c * pps + pg] = _mm(qbf, kp, NT_DIMS)
            part = jnp.sum(kp.reshape(page * HEADS // SUBLANES, SUBLANES, DH), axis=0)
            ksum = part if ksum is None else ksum + part
        total = ksum
        for sh in range(1, fold):
            total = total + pltpu.roll(ksum, sh * HEADS, axis=0)
        ksum_ref[pl.ds(pl.multiple_of((c * (pps // ppb) + jb) * SUBLANES, SUBLANES), SUBLANES), :] = total

    @pl.when(c == nc - 1)
    def _():
        ht = HEADS * t
        kmean = ksum_ref[...] * (1.0 / MOBA_BLOCK)
        gate = _mm3(qall, kmean, NT_DIMS)
        grow = lax.broadcasted_iota(I32, gate.shape, 0)
        gcol = lax.broadcasted_iota(I32, gate.shape, 1)
        gate = jnp.where(gcol % SUBLANES == grow // t, gate, NEG_INF)
        picks = [(idx // SUBLANES, val > NEG_INF)
                 for idx, val in _top_picks(gate, MOBA_TOPK, gcol, n_blk * SUBLANES, 1)]

        def block_on(j):
            on = picks[0][1] & (picks[0][0] == j)
            for blk_id, ok in picks[1:]:
                on = on | (ok & (blk_id == j))
            return on

        s_own = _mm(qbf, kn_ref[...], NT_DIMS)
        ro = lax.broadcasted_iota(I32, s_own.shape, 0)
        co = lax.broadcasted_iota(I32, s_own.shape, 1)
        s_own = jnp.where((co % HEADS == ro // t) & (co // HEADS <= ro % t), s_own, NEG_INF)
        rp = lax.broadcasted_iota(I32, (ht, page * HEADS), 0)
        cp = lax.broadcasted_iota(I32, (ht, page * HEADS), 1)
        same_head = cp % HEADS == rp // t
        m = jnp.max(s_own, axis=1, keepdims=True)
        for pg in range(n_pages):
            keep = same_head & block_on(pg // ppb)
            m = jnp.maximum(m, jnp.max(jnp.where(keep, s_ref[pg], NEG_INF), axis=1, keepdims=True))
        p_own = jnp.exp(s_own - m)
        l = jnp.sum(p_own, axis=1, keepdims=True)
        for pg in range(n_pages):
            keep = same_head & block_on(pg // ppb)
            p = jnp.where(keep, jnp.exp(s_ref[pg] - m), 0.0)
            s_ref[pg] = p
            l = l + jnp.sum(p, axis=1, keepdims=True)
        inv = 1.0 / l
        po_ref[0] = p_own * inv
        for pg in range(n_pages):
            pp_ref[0, pg] = (s_ref[pg] * inv).astype(pp_ref.dtype)


def _moba_sample_apply_kernel(pt_ref, pp_ref, po_ref, vn_ref, *rest, t, pps):
    del pt_ref
    v_refs = rest[:pps]
    o_ref, acc_ref = rest[pps:]
    c = pl.program_id(1)
    nc = pl.num_programs(1)

    @pl.when(c == 0)
    def _():
        acc_ref[...] = _mm(po_ref[0], vn_ref[...])

    acc = acc_ref[...]
    for pg in range(pps):
        acc = acc + _mm(pp_ref[0, pg], v_refs[pg][0])
    acc_ref[...] = acc

    @pl.when(c == nc - 1)
    def _():
        for h in range(HEADS):
            o_ref[:, h * DH:(h + 1) * DH] = acc_ref[h * t:(h + 1) * t, :]


def _moba_sample(q, k_new, v_new, cache_k, cache_v, page_table):
    bd, n_pages = page_table.shape
    n = q.shape[0]
    t = n // bd
    prow = cache_k.shape[1]
    page = prow // HEADS
    pps = min(PAGES_PER_STEP, n_pages)
    assert n_pages % pps == 0 and (n_pages * page) % MOBA_BLOCK == 0 and MOBA_BLOCK % page == 0
    assert SUBLANES % HEADS == 0 and pps % (MOBA_BLOCK // page) == 0
    nc = n_pages // pps
    ht = HEADS * t
    n_blk = n_pages * page // MOBA_BLOCK
    pt = page_table.reshape(-1).astype(I32)
    seq_q = pl.BlockSpec((t, HW), lambda b, c, pt: (b, 0))
    seq_kv = pl.BlockSpec((ht, DH), lambda b, c, pt: (b, 0))
    own = pl.BlockSpec((1, ht, ht), lambda b, c, pt: (b, 0, 0))

    def page_spec(i):
        return pl.BlockSpec((1, prow, DH), lambda b, c, pt: (pt[b * n_pages + c * pps + i], 0, 0))

    p_past, p_own = pl.pallas_call(
        functools.partial(_moba_sample_scores_kernel, t=t, n_pages=n_pages, page=page, pps=pps),
        grid_spec=pltpu.PrefetchScalarGridSpec(
            num_scalar_prefetch=1,
            grid=(bd, nc),
            in_specs=[seq_q, seq_kv] + [page_spec(i) for i in range(pps)],
            out_specs=[pl.BlockSpec((1, n_pages, ht, prow), lambda b, c, pt: (b, 0, 0, 0)), own],
            scratch_shapes=[pltpu.VMEM((n_pages, ht, prow), F32),
                            pltpu.VMEM((n_blk * SUBLANES, DH), F32)]),
        out_shape=[jax.ShapeDtypeStruct((bd, n_pages, ht, prow), BF16),
                   jax.ShapeDtypeStruct((bd, ht, ht), F32)],
        compiler_params=_params("parallel", "arbitrary"),
        name="moba_sample_scores",
    )(pt, q, k_new, *([cache_k] * pps))

    return pl.pallas_call(
        functools.partial(_moba_sample_apply_kernel, t=t, pps=pps),
        grid_spec=pltpu.PrefetchScalarGridSpec(
            num_scalar_prefetch=1,
            grid=(bd, nc),
            in_specs=[pl.BlockSpec((1, pps, ht, prow), lambda b, c, pt: (b, c, 0, 0)), own, seq_kv]
                     + [page_spec(i) for i in range(pps)],
            out_specs=seq_q,
            scratch_shapes=[pltpu.VMEM((ht, DH), F32)]),
        out_shape=jax.ShapeDtypeStruct((n, HW), F32),
        compiler_params=_params("parallel", "arbitrary"),
        name="moba_sample_apply",
    )(pt, p_past, p_own, v_new, *([cache_v] * pps))


def _postmix_kernel(dn_ref, mb_ref, x_ref, wo1_ref, wo2_ref, nx_ref, wq_ref, mk_ref, mv_ref, wxo_ref,
                    nf_ref, wr_ref, br_ref, tri_ref,
                    x2_ref, h2_ref, tope_ref, gate_ref, rank_ref, cnt_ref, carry_ref, *, seqs, n_exp):
    i = pl.program_id(0)

    @pl.when(i == 0)
    def _():
        carry_ref[...] = jnp.zeros_like(carry_ref)

    x1 = x_ref[...] + _mm(dn_ref[...], wo1_ref[...]) + _mm(mb_ref[...], wo2_ref[...])
    tm = x1.shape[0]
    q = _mm(_rms(x1, nx_ref[...]), wq_ref[...])
    scale = DH ** -0.5
    heads = []
    for h in range(HEADS):
        hs = slice(h * DH, (h + 1) * DH)
        if seqs == 1:
            s = _mm(q[:, hs], mk_ref[0, :, hs], NT_DIMS) * scale
            p = jnp.exp(s - jnp.max(s, axis=-1, keepdims=True))
            p = p / jnp.sum(p, axis=-1, keepdims=True)
            heads.append(_mm(p, mv_ref[0, :, hs]))
        else:
            q3 = q[:, hs].reshape(seqs, tm // seqs, DH).astype(BF16)
            s = jnp.einsum("gtd,gmd->gtm", q3, mk_ref[:, :, hs].astype(BF16), preferred_element_type=F32) * scale
            p = jnp.exp(s - jnp.max(s, axis=-1, keepdims=True))
            p = p / jnp.sum(p, axis=-1, keepdims=True)
            o = jnp.einsum("gtm,gmd->gtd", p.astype(BF16), mv_ref[:, :, hs].astype(BF16), preferred_element_type=F32)
            heads.append(o.reshape(tm, DH))
    x2 = x1 + _mm(jnp.concatenate(heads, axis=1), wxo_ref[...])
    x2_ref[...] = x2
    h2 = _rms(x2, nf_ref[...])
    _store_row_tiles(h2_ref, h2)

    logits = _mm3(h2, wr_ref[...]) + br_ref[...]
    lt = logits.T[:n_exp]
    erow = lax.broadcasted_iota(I32, lt.shape, 0)
    picks = _top_picks(lt, TOP_K, erow, n_exp, 0)
    ex = [jnp.exp(val - picks[0][1]) for _, val in picks]
    den = ex[0]
    for e in ex[1:]:
        den = den + e
    onehot = [erow == idx for idx, _ in picks]
    any_hot = onehot[0]
    for oh in onehot[1:]:
        any_hot = any_hot | oh
    hot = any_hot.astype(F32)
    base = _mm(hot, tri_ref[...]) + carry_ref[:, 0:1]
    for k in range(TOP_K):
        tope_ref[k:k + 1, :] = picks[k][0]
        gate_ref[k:k + 1, :] = ex[k] / den
        rank_ref[k:k + 1, :] = jnp.sum(jnp.where(onehot[k], base, 0.0), axis=0, keepdims=True).astype(I32)
    pad = SUBLANES - TOP_K
    tope_ref[TOP_K:, :] = jnp.zeros((pad, tm), I32)
    gate_ref[TOP_K:, :] = jnp.zeros((pad, tm), F32)
    rank_ref[TOP_K:, :] = jnp.zeros((pad, tm), I32)
    carry = carry_ref[...] + jnp.sum(hot, axis=1, keepdims=True)
    carry_ref[...] = carry
    cnt_ref[...] = carry.astype(I32)


def _post_mix(dn, mb, x, mem_k, mem_v, w_out, norm_x, w_xq, w_xo, norm_ffn, w_router, b_router):
    n, d = x.shape
    n_seq, n_mem, _ = mem_k.shape
    rows_per_seq = n // n_seq
    n_exp = w_router.shape[1]
    if rows_per_seq >= ROW_TILE:
        tm, seqs = ROW_TILE, 1
        assert rows_per_seq % tm == 0
        mem_map = lambda i: (i * tm // rows_per_seq, 0, 0)
    else:
        seqs = min(max(SAMPLE_SEQS, LANES // rows_per_seq), n_seq)
        tm = seqs * rows_per_seq
        mem_map = lambda i: (i, 0, 0)
    wo = w_out.astype(BF16)
    wr = jnp.pad(w_router, ((0, 0), (0, LANES - n_exp)))
    br = jnp.pad(b_router, (0, LANES - n_exp)).reshape(1, LANES)
    tri = (jnp.arange(tm)[:, None] < jnp.arange(tm)[None, :]).astype(BF16)
    row = lambda w: pl.BlockSpec((tm, w), lambda i: (i, 0))
    col = pl.BlockSpec((SUBLANES, tm), lambda i: (0, i))
    mem_spec = pl.BlockSpec((seqs, n_mem, HW), mem_map)
    return pl.pallas_call(
        functools.partial(_postmix_kernel, seqs=seqs, n_exp=n_exp),
        grid=(n // tm,),
        in_specs=[row(HW), row(HW), row(d), _full((HW, d)), _full((HW, d)), _full((1, d)), _full((d, HW)),
                  mem_spec, mem_spec, _full((HW, d)), _full((1, d)), _full((d, LANES)), _full((1, LANES)),
                  _full((tm, tm))],
        out_specs=[row(d), pl.BlockSpec((tm * (d // LANES), LANES), lambda i: (i, 0)), col, col, col,
                   _full((n_exp, LANES))],
        out_shape=[jax.ShapeDtypeStruct((n, d), F32), jax.ShapeDtypeStruct((n * (d // LANES), LANES), F32),
                   jax.ShapeDtypeStruct((SUBLANES, n), I32), jax.ShapeDtypeStruct((SUBLANES, n), F32),
                   jax.ShapeDtypeStruct((SUBLANES, n), I32), jax.ShapeDtypeStruct((n_exp, LANES), I32)],
        scratch_shapes=[pltpu.VMEM((n_exp, LANES), F32)],
        compiler_params=_params("arbitrary"),
        name="post_mix",
    )(dn, mb, x, wo[:HW], wo[HW:], norm_x.reshape(1, d), w_xq.astype(BF16), mem_k, mem_v,
      w_xo.astype(BF16), norm_ffn.reshape(1, d), wr, br, tri)


def _store_row_tiles(ref, x):
    rows, width = x.shape
    rt = width // LANES
    for c in range(rt):
        ref[pl.ds(c, rows, stride=rt), :] = x[:, c * LANES:(c + 1) * LANES]


def _load_row_tiles(ref, rows):
    rt = ref.shape[0] // rows
    return jnp.concatenate([ref[pl.ds(c, rows, stride=rt), :] for c in range(rt)], axis=1)


def _plan_kernel(cnt_ref, tope_ref, rank_ref, dest_ref, be_ref, nv_ref, *, n_exp, bm, n_blocks):
    tope = tope_ref[...]
    dest = rank_ref[...]
    start = jnp.int32(0)
    first = pl.program_id(0) == 0
    for e in range(n_exp):
        dest = dest + jnp.where(tope == e, start * bm, 0)
        cnt_e = cnt_ref[e]
        nb_e = (cnt_e + (bm - 1)) // bm

        @pl.when(first)
        def _(e=e, start=start, nb_e=nb_e, cnt_e=cnt_e):
            def fill(j, c):
                be_ref[start + j] = e
                nv_ref[start + j] = jnp.minimum(cnt_e - j * bm, bm)
                return c
            lax.fori_loop(0, nb_e, fill, 0)

        start = start + nb_e
    dest_ref[...] = dest

    @pl.when(first)
    def _():
        def fill(j, c):
            be_ref[j] = n_exp - 1
            nv_ref[j] = 0
            return c
        lax.fori_loop(start, n_blocks, fill, 0)


def _moe_plan(cnt, tope, rank, bm, n_blocks):
    n = tope.shape[1]
    n_exp = cnt.shape[0]
    tl = min(2048, n)
    col = pl.BlockSpec((SUBLANES, tl), lambda i: (0, i))
    smem = lambda: pl.BlockSpec(memory_space=pltpu.SMEM)
    return pl.pallas_call(
        functools.partial(_plan_kernel, n_exp=n_exp, bm=bm, n_blocks=n_blocks),
        grid=(n // tl,),
        in_specs=[smem(), col, col],
        out_specs=[col, smem(), smem()],
        out_shape=[jax.ShapeDtypeStruct((SUBLANES, n), I32), jax.ShapeDtypeStruct((n_blocks,), I32),
                   jax.ShapeDtypeStruct((n_blocks,), I32)],
        compiler_params=_params("arbitrary"),
        name="moe_plan",
    )(cnt[:, 0], tope, rank)


def _tile_copy(src, src_row, dst, dst_row, sem):
    return pltpu.make_async_copy(src.at[pl.ds(src_row * SUBLANES, SUBLANES), :],
                                 dst.at[pl.ds(dst_row * SUBLANES, SUBLANES), :], sem)


def _dispatch_kernel(cnt_ref, dest_ref, x_ref, xs_ref, zero_ref, sem, zsem, *, tm, bm, n_exp, n_blocks):
    @pl.when(pl.program_id(0) == 0)
    def _():
        zero_ref[...] = jnp.zeros_like(zero_ref)

        def block_fill(j):
            return pltpu.make_async_copy(zero_ref, xs_ref.at[pl.ds(j * (bm * SUBLANES), bm * SUBLANES), :], zsem)

        def fills(act):
            start = jnp.int32(0)
            for e in range(n_exp):
                nb_e = (cnt_ref[e] + (bm - 1)) // bm
                start = start + nb_e

                @pl.when(nb_e > 0)
                def _(last=start - 1):
                    act(block_fill(last))

            def tail(j, c):
                act(block_fill(j))
                return c
            lax.fori_loop(start, n_blocks, tail, 0)

        fills(lambda cp: cp.start())
        fills(lambda cp: cp.wait())

    def issue(t, c):
        for k in range(TOP_K):
            _tile_copy(x_ref, t, xs_ref, dest_ref[k, t], sem).start()
        return c

    lax.fori_loop(0, tm, issue, 0, unroll=2)

    def drain(t, c):
        for k in range(TOP_K):
            _tile_copy(x_ref, t, xs_ref, dest_ref[k, t], sem).wait()
        return c

    lax.fori_loop(0, tm, drain, 0)


def _moe_dispatch(h2t, dest, cnt, bm, n_blocks):
    n = h2t.shape[0] // SUBLANES
    tm = min(MOVE_TILE, n)
    n_exp = cnt.shape[0]
    return pl.pallas_call(
        functools.partial(_dispatch_kernel, tm=tm, bm=bm, n_exp=n_exp, n_blocks=n_blocks),
        grid=(n // tm,),
        in_specs=[pl.BlockSpec(memory_space=pltpu.SMEM),
                  pl.BlockSpec((SUBLANES, tm), lambda i: (0, i), memory_space=pltpu.SMEM),
                  pl.BlockSpec((tm * SUBLANES, LANES), lambda i: (i, 0))],
        out_specs=pl.BlockSpec(memory_space=pl.ANY),
        out_shape=jax.ShapeDtypeStruct((n_blocks * bm * SUBLANES, LANES), F32),
        scratch_shapes=[pltpu.VMEM((bm * SUBLANES, LANES), F32), pltpu.SemaphoreType.DMA, pltpu.SemaphoreType.DMA],
        compiler_params=_params("arbitrary"),
        name="moe_dispatch",
    )(cnt[:, 0], dest, h2t)


def _expert_kernel(be_ref, nv_ref, x_ref, wg_ref, bg_ref, wu_ref, bu_ref, wd_ref, bd_ref, y_ref, *, bm):
    del be_ref
    n_valid = nv_ref[pl.program_id(0)]

    @pl.when(n_valid > 0)
    def _():
        x = _load_row_tiles(x_ref, bm).astype(BF16)
        g = jnp.minimum(_mm(x, wg_ref[0]) + bg_ref[0], SWIGLU_LIMIT)
        u = jnp.clip(_mm(x, wu_ref[0]) + bu_ref[0], -SWIGLU_LIMIT, SWIGLU_LIMIT)
        a = g * _sigmoid(SWIGLU_ALPHA * g) * (u + 1.0)
        _store_row_tiles(y_ref, _mm(a, wd_ref[0]) + bd_ref[0])

    @pl.when(n_valid <= 0)
    def _():
        y_ref[...] = jnp.zeros_like(y_ref)


def _moe_experts(xs, block_e, n_valid, w_gate, b_gate, w_up, b_up, w_down, b_down, bm):
    n_exp, d, d_ff = w_gate.shape
    assert d == SUBLANES * LANES, "a model row must be exactly one (SUBLANES, LANES) tile"
    n_blocks = xs.shape[0] // (bm * SUBLANES)
    wspec = lambda r, c: pl.BlockSpec((1, r, c), lambda i, be, nv: (be[i], 0, 0))
    rows = pl.BlockSpec((bm * SUBLANES, LANES), lambda i, be, nv: (i, 0))
    return pl.pallas_call(
        functools.partial(_expert_kernel, bm=bm),
        grid_spec=pltpu.PrefetchScalarGridSpec(
            num_scalar_prefetch=2,
            grid=(n_blocks,),
            in_specs=[rows, wspec(d, d_ff), wspec(1, d_ff), wspec(d, d_ff), wspec(1, d_ff), wspec(d_ff, d), wspec(1, d)],
            out_specs=rows),
        out_shape=jax.ShapeDtypeStruct(xs.shape, F32),
        compiler_params=_params("arbitrary"),
        name="moe_experts",
    )(block_e, n_valid, xs, w_gate.astype(BF16), b_gate.reshape(n_exp, 1, d_ff), w_up.astype(BF16),
      b_up.reshape(n_exp, 1, d_ff), w_down.astype(BF16), b_down.reshape(n_exp, 1, d))


def _combine_kernel(dest_ref, gate_ref, x_ref, nw_ref, ys_ref, o_ref, buf_ref, sem, *, tm):
    def issue(t, c):
        for k in range(TOP_K):
            _tile_copy(ys_ref, dest_ref[k, t], buf_ref.at[k], t, sem).start()
        return c

    lax.fori_loop(0, tm, issue, 0, unroll=2)

    def drain(t, c):
        for k in range(TOP_K):
            _tile_copy(ys_ref, dest_ref[k, t], buf_ref.at[k], t, sem).wait()
        return c

    lax.fori_loop(0, tm, drain, 0)
    gt = gate_ref[...].T
    acc = x_ref[...]
    for k in range(TOP_K):
        acc = acc + gt[:, k:k + 1] * _load_row_tiles(buf_ref.at[k], tm)
    o_ref[...] = _rms(acc, nw_ref[...])


def _moe_combine(ys, dest, gates, x2, norm_final):
    n, d = x2.shape
    tm = min(MOVE_TILE, n)
    col = lambda ms: pl.BlockSpec((SUBLANES, tm), lambda i: (0, i), memory_space=ms)
    return pl.pallas_call(
        functools.partial(_combine_kernel, tm=tm),
        grid=(n // tm,),
        in_specs=[col(pltpu.SMEM), col(pltpu.VMEM), pl.BlockSpec((tm, d), lambda i: (i, 0)), _full((1, d)),
                  pl.BlockSpec(memory_space=pl.ANY)],
        out_specs=pl.BlockSpec((tm, d), lambda i: (i, 0)),
        out_shape=jax.ShapeDtypeStruct((n, d), F32),
        scratch_shapes=[pltpu.VMEM((TOP_K, tm * SUBLANES, LANES), F32), pltpu.SemaphoreType.DMA],
        compiler_params=_params("arbitrary"),
        name="moe_combine",
    )(dest, gates, x2, norm_final.reshape(1, d), ys)


def _moe_and_final_norm(x2, h2t, tope, gates, rank, cnt, w_gate, b_gate, w_up, b_up, w_down, b_down, norm_final):
    n = x2.shape[0]
    n_exp = w_gate.shape[0]
    bm = MOE_BLOCK_ROWS
    n_blocks = -(-(n * TOP_K + n_exp * (bm - 1)) // bm)
    dest, block_e, n_valid = _moe_plan(cnt, tope, rank, bm, n_blocks)
    xs = _moe_dispatch(h2t, dest, cnt, bm, n_blocks)
    ys = _moe_experts(xs, block_e, n_valid, w_gate, b_gate, w_up, b_up, w_down, b_down, bm)
    return _moe_combine(ys, dest, gates, x2, norm_final)


def kernel(x_prompt, x_sample, mem_prompt, cache_k, cache_v, page_table, state_delta, state_conv, cache_mem_k, cache_mem_v, norm_mix, w_in, conv_w, a_log, dt_bias, dn_norm, w_out, norm_x, norm_mem, w_xq, w_xk, w_xv, w_xo, norm_ffn, w_router, b_router, w_gate, b_gate, w_up, b_up, w_down, b_down, norm_final):
    depth = w_in.shape[0]
    assert depth == 1, "single-layer stack"
    b, s, d = x_prompt.shape
    bd, t, _ = x_sample.shape
    n_mem = mem_prompt.shape[1]
    n_pool, page = cache_k.shape[1], cache_k.shape[2]
    past = page_table.shape[1] * page
    assert past % MOBA_BLOCK == 0, "past length must be whole MoBA blocks"
    l = 0
    moe_w = (w_gate[l], b_gate[l], w_up[l], b_up[l], w_down[l], b_down[l])

    xs = x_sample.reshape(bd * t, d)
    tile_s = min(ROW_TILE, bd * t)
    cos_s, sin_s = _rope_tables(past + jnp.arange(tile_s) % t)
    conv_s, z_s, gb_s, q_s, k_s, v_s = _in_proj(xs, norm_mix[l], w_in[l], a_log[l], dt_bias[l],
                                                cos_s, sin_s, prompt=False)
    dn_s, s_s, cv_s = _deltanet_sample(conv_s, z_s, gb_s, state_conv[l], state_delta[l], conv_w[l], dn_norm[l])
    k4_s = k_s.reshape(bd * t * HEADS, DH)
    v4_s = v_s.reshape(bd * t * HEADS, DH)
    mb_s = _moba_sample(q_s, k4_s, v4_s, cache_k.reshape(n_pool, page * HEADS, DH),
                        cache_v.reshape(n_pool, page * HEADS, DH), page_table)
    x2s, h2s, tope_s, gates_s, rank_s, cnt_s = _post_mix(
        dn_s, mb_s, xs, cache_mem_k.reshape(bd, n_mem, HW), cache_mem_v.reshape(bd, n_mem, HW),
        w_out[l], norm_x[l], w_xq[l], w_xo[l], norm_ffn[l], w_router[l], b_router[l])
    y_sample = _moe_and_final_norm(x2s, h2s, tope_s, gates_s, rank_s, cnt_s, *moe_w, norm_final)

    xp = x_prompt.reshape(b * s, d)
    cos_p, sin_p = _rope_tables(jnp.arange(s))
    conv_in, z, gb, q, k, v, kb, vt, kmean = _in_proj(xp, norm_mix[l], w_in[l], a_log[l], dt_bias[l],
                                                      cos_p, sin_p, prompt=True)
    dn, s_p, cv_p = _deltanet_prompt(conv_in, z, gb, conv_w[l], dn_norm[l], b)
    mb = _moba_prompt(q, kb, vt, kmean, b)
    mk_p, mv_p = _mem_kv(mem_prompt.reshape(b * n_mem, d), norm_mem[l], w_xk[l], w_xv[l])
    x2, h2, tope, gates, rank, cnt = _post_mix(
        dn, mb, xp, mk_p.reshape(b, n_mem, HW), mv_p.reshape(b, n_mem, HW), w_out[l], norm_x[l], w_xq[l],
        w_xo[l], norm_ffn[l], w_router[l], b_router[l])
    y_prompt = _moe_and_final_norm(x2, h2, tope, gates, rank, cnt, *moe_w, norm_final)

    hd = (HEADS, DH)
    return (y_prompt.reshape(b, s, d), y_sample.reshape(bd, t, d),
            k.reshape(1, b, s, *hd), v.reshape(1, b, s, *hd),
            k4_s.reshape(1, bd, t, *hd), v4_s.reshape(1, bd, t, *hd),
            s_p[None], s_s[None], cv_p[None], cv_s[None],
            mk_p.reshape(1, b, n_mem, *hd), mv_p.reshape(1, b, n_mem, *hd))
```

```python
import functools
import math

import jax
import jax.numpy as jnp
from jax import lax
from jax.experimental import pallas as pl
from jax.experimental.pallas import tpu as pltpu

F32 = jnp.float32
BF16 = jnp.bfloat16
I32 = jnp.int32

HEADS = 4
DH = 128
HW = HEADS * DH
CONV_W = 4
DN_CHUNK = 64
MOBA_BLOCK = 256
MOBA_TOPK = 3
ROPE_THETA = 10000.0
TOP_K = 4
SWIGLU_ALPHA = 1.702
SWIGLU_LIMIT = 7.0
EPS = 1e-6
NEG_INF = float("-inf")
LOG2_E = math.log2(math.e)

LANES = 128
SUBLANES = 8
VMEM_LIMIT = 56 * 1024 * 1024

ROW_TILE = 512
DN_GROUP = 256
MOBA_GROUP = 4
MOE_BLOCK_ROWS = 512
MOVE_TILE = 256
PAGES_PER_STEP = 16
SAMPLE_SEQS = 8

NN_DIMS = (((1,), (0,)), ((), ()))
NT_DIMS = (((1,), (1,)), ((), ()))
TN_DIMS = (((0,), (0,)), ((), ()))


def _mm(a, b, dims=NN_DIMS):
    return lax.dot_general(a.astype(BF16), b.astype(BF16), dims, preferred_element_type=F32)


def _split(a):
    hi = a.astype(BF16)
    lo = (a - hi.astype(F32)).astype(BF16)
    return hi, lo


def _mm3(a, b, dims=NN_DIMS):
    a_hi, a_lo = _split(a)
    b_hi, b_lo = _split(b)
    d = functools.partial(lax.dot_general, dimension_numbers=dims, preferred_element_type=F32)
    return d(a_hi, b_hi) + (d(a_hi, b_lo) + d(a_lo, b_hi))


def _rms(x, w):
    return x * lax.rsqrt(jnp.mean(x * x, axis=-1, keepdims=True) + EPS) * w


def _sigmoid(x):
    return 1.0 / (1.0 + jnp.exp(-x))


def _softplus(x):
    return jnp.maximum(x, 0.0) + jnp.log1p(jnp.exp(-jnp.abs(x)))


def _params(*sem):
    return pltpu.CompilerParams(dimension_semantics=sem, vmem_limit_bytes=VMEM_LIMIT)


def _full(shape):
    nd = len(shape)
    return pl.BlockSpec(shape, lambda *_: (0,) * nd)


def _top_picks(scores, n_pick, ids, n_ids, axis):
    picks = []
    cur = scores
    for _ in range(n_pick):
        m = jnp.max(cur, axis=axis, keepdims=True)
        idx = jnp.min(jnp.where(cur == m, ids, n_ids), axis=axis, keepdims=True)
        picks.append((idx, m))
        cur = jnp.where(ids == idx, NEG_INF, cur)
    return picks


def _memkv_kernel(m_ref, nw_ref, wk_ref, wv_ref, k_ref, v_ref):
    h = _rms(m_ref[...], nw_ref[...])
    rows = h.shape[0]
    k = _mm(h, wk_ref[...])
    v = _mm(h, wv_ref[...])
    for hd in range(HEADS):
        k_ref[pl.ds(hd, rows, stride=HEADS), :] = k[:, hd * DH:(hd + 1) * DH]
        v_ref[pl.ds(hd, rows, stride=HEADS), :] = v[:, hd * DH:(hd + 1) * DH]


def _mem_kv(mem, norm_mem, w_xk, w_xv):
    n, d = mem.shape
    tm = min(ROW_TILE, n)
    return pl.pallas_call(
        _memkv_kernel,
        grid=(n // tm,),
        in_specs=[pl.BlockSpec((tm, d), lambda i: (i, 0)), _full((1, d)), _full(w_xk.shape), _full(w_xv.shape)],
        out_specs=[pl.BlockSpec((tm * HEADS, DH), lambda i: (i, 0))] * 2,
        out_shape=[jax.ShapeDtypeStruct((n * HEADS, DH), F32)] * 2,
        compiler_params=_params("parallel"),
        name="mem_kv",
    )(mem, norm_mem.reshape(1, d), w_xk.astype(BF16), w_xv.astype(BF16))


def _rope(x, cos, sin):
    return x * cos + pltpu.roll(x, DH // 2, axis=1) * sin


def _inproj_kernel(x_ref, nw_ref, w_ref, wab_ref, cos_ref, sin_ref, alog_ref, dtb_ref,
                   conv_ref, z_ref, gb_ref, q_ref, k_ref, v_ref, *rest, conv_ch, prompt):
    x = x_ref[...]
    hb = _rms(x, nw_ref[...]).astype(BF16)
    conv_ref[...] = _mm(hb, w_ref[:, 0:conv_ch])
    z_ref[...] = _mm(hb, w_ref[:, conv_ch:conv_ch + HW])
    ab = _mm(hb, wab_ref[...])
    lane = lax.broadcasted_iota(I32, ab.shape, 1)
    g = -jnp.exp(alog_ref[...]) * _softplus(ab + dtb_ref[...])
    gb_ref[...] = jnp.where(lane < HEADS, g, _sigmoid(ab))
    off = conv_ch + HW
    cos = cos_ref[...]
    sin = sin_ref[...]
    q = _mm(hb, w_ref[:, off:off + HW])
    k = _mm(hb, w_ref[:, off + HW:off + 2 * HW])
    v = _mm(hb, w_ref[:, off + 2 * HW:off + 3 * HW])
    rows = x.shape[0]
    k_rot = []
    for h in range(HEADS):
        sl = slice(h * DH, (h + 1) * DH)
        q_ref[:, sl] = _rope(q[:, sl], cos, sin)
        k_rot.append(_rope(k[:, sl], cos, sin))
        k_ref[pl.ds(h, rows, stride=HEADS), :] = k_rot[h]
        v_ref[pl.ds(h, rows, stride=HEADS), :] = v[:, sl]
    if prompt:
        kb_ref, vt_ref, kmean_ref = rest
        for h in range(HEADS):
            kb_ref[:, h * DH:(h + 1) * DH] = k_rot[h].astype(BF16)
        for blk in range(x.shape[0] // MOBA_BLOCK):
            rows = slice(blk * MOBA_BLOCK, (blk + 1) * MOBA_BLOCK)
            vt_ref[blk] = v[rows].T.astype(BF16)
            for h in range(HEADS):
                kmean_ref[blk, :, h * DH:(h + 1) * DH] = jnp.mean(k_rot[h][rows], axis=0, keepdims=True)


def _in_proj(x, norm_w, w_in, a_log, dt_bias, cos, sin, *, prompt):
    n, d = x.shape
    conv_ch = 3 * HW
    tm = min(ROW_TILE, n)
    off_a = conv_ch + HW
    w_main = jnp.concatenate([w_in[:, :off_a], w_in[:, off_a + 2 * HEADS:]], axis=1).astype(BF16)
    w_ab = jnp.pad(w_in[:, off_a:off_a + 2 * HEADS], ((0, 0), (0, LANES - 2 * HEADS))).astype(BF16)
    alog = jnp.pad(a_log, (0, LANES - HEADS)).reshape(1, LANES)
    dtb = jnp.pad(dt_bias, (0, LANES - HEADS)).reshape(1, LANES)
    n_pos = cos.shape[0] // tm
    row = lambda w: pl.BlockSpec((tm, w), lambda i: (i, 0))
    kv_spec = pl.BlockSpec((tm * HEADS, DH), lambda i: (i, 0))
    out_specs = [row(conv_ch), row(HW), row(LANES), row(HW), kv_spec, kv_spec]
    out_shape = ([jax.ShapeDtypeStruct((n, w), F32) for w in (conv_ch, HW, LANES, HW)]
                 + [jax.ShapeDtypeStruct((n * HEADS, DH), F32)] * 2)
    if prompt:
        nb = tm // MOBA_BLOCK
        out_specs += [row(HW),
                      pl.BlockSpec((nb, HW, MOBA_BLOCK), lambda i: (i, 0, 0)),
                      pl.BlockSpec((nb, 1, HW), lambda i: (i, 0, 0))]
        out_shape += [jax.ShapeDtypeStruct((n, HW), BF16),
                      jax.ShapeDtypeStruct((n // MOBA_BLOCK, HW, MOBA_BLOCK), BF16),
                      jax.ShapeDtypeStruct((n // MOBA_BLOCK, 1, HW), F32)]
    return pl.pallas_call(
        functools.partial(_inproj_kernel, conv_ch=conv_ch, prompt=prompt),
        grid=(n // tm,),
        in_specs=[row(d), _full((1, d)), _full(w_main.shape), _full(w_ab.shape),
                  pl.BlockSpec((tm, DH), lambda i: (i % n_pos, 0)),
                  pl.BlockSpec((tm, DH), lambda i: (i % n_pos, 0)),
                  _full((1, LANES)), _full((1, LANES))],
        out_specs=out_specs,
        out_shape=out_shape,
        compiler_params=_params("parallel"),
        name="in_proj",
    )(x, norm_w.reshape(1, d), w_main, w_ab, cos, sin, alog, dtb)


def _rope_tables(pos):
    half = DH // 2
    inv = ROPE_THETA ** (-jnp.arange(half, dtype=F32) / half)
    ang = pos.astype(F32)[:, None] * inv[None, :]
    c, s = jnp.cos(ang), jnp.sin(ang)
    return jnp.concatenate([c, c], axis=1), jnp.concatenate([-s, s], axis=1)


def _chunk_cumsum(g, row_in_chunk):
    for sh in (1, 2, 4, 8, 16, 32):
        g = g + jnp.where(row_in_chunk >= sh, pltpu.roll(g, sh, axis=0), 0.0)
    return g


def _unit_lower_inverse(a):
    n = a.shape[0]
    eye = (lax.broadcasted_iota(I32, (n, n), 0) == lax.broadcasted_iota(I32, (n, n), 1)).astype(F32)
    p = eye - a
    ak = a
    for _ in range(5):
        ak = _mm(ak, ak)
        p = p + _mm(p, ak)
    return p


def _gated_out(o, z, dn_w):
    return _rms(o, dn_w) * (z * _sigmoid(z))


def _dn_prompt_kernel(u_ref, halo_ref, z_ref, gb_ref, cw_ref, dnw_ref,
                      o_ref, sfin_ref, cnew_ref,
                      ext_ref, s_ref, wq_ref, us_ref, kd_ref, vn_ref, oi_ref, qk_ref, gl_ref, *, tm):
    i = pl.program_id(1)
    nt = pl.num_programs(1)
    nc = tm // DN_CHUNK
    cpg = DN_GROUP // DN_CHUNK
    hal = CONV_W - 1

    @pl.when(i == 0)
    def _():
        s_ref[...] = jnp.zeros_like(s_ref)

    u = u_ref[...]
    ext_ref[0:SUBLANES, :] = jnp.where(i > 0, halo_ref[...], 0.0)
    ext_ref[SUBLANES:SUBLANES + tm, :] = u
    cw = cw_ref[...]
    y = u * cw[hal:hal + 1]
    for j in range(hal):
        y = y + ext_ref[SUBLANES - hal + j:SUBLANES - hal + j + tm, :] * cw[j:j + 1]
    qkv = y * _sigmoid(y)

    gb = gb_ref[...]
    row_in_chunk = lax.broadcasted_iota(I32, gb.shape, 0) % DN_CHUNK
    gc = _chunk_cumsum(gb, row_in_chunk)
    gc3 = gc.reshape(nc, DN_CHUNK, LANES)
    glast = jnp.broadcast_to(gc3[:, DN_CHUNK - 1:DN_CHUNK, :], gc3.shape).reshape(tm, LANES)
    gl_ref[...] = jnp.exp(glast)
    gct = gc.T

    gi = lax.broadcasted_iota(I32, (DN_GROUP, DN_GROUP), 0)
    gj = lax.broadcasted_iota(I32, (DN_GROUP, DN_GROUP), 1)
    same = (gi // DN_CHUNK) == (gj // DN_CHUNK)
    causal = same & (gi >= gj)
    strict = same & (gi > gj)

    for h in range(HEADS):
        qh = qkv[:, h * DH:(h + 1) * DH]
        kh = qkv[:, HW + h * DH:HW + (h + 1) * DH]
        vh = qkv[:, 2 * HW + h * DH:2 * HW + (h + 1) * DH]
        qn = qh * lax.rsqrt(jnp.sum(qh * qh, axis=-1, keepdims=True) + EPS) * (DH ** -0.5)
        kn = kh * lax.rsqrt(jnp.sum(kh * kh, axis=-1, keepdims=True) + EPS)
        gcol = gc[:, h:h + 1]
        beta = gb[:, HEADS + h:HEADS + h + 1]
        eg = jnp.exp(gcol)
        kbeta = kn * beta
        kd_ref[h] = kn * jnp.exp(glast[:, h:h + 1] - gcol)
        qe = qn * eg
        for r in range(tm // DN_GROUP):
            rs = slice(r * DN_GROUP, (r + 1) * DN_GROUP)
            decay = jnp.where(causal, jnp.exp(jnp.minimum(gcol[rs] - gct[h:h + 1, rs], 0.0)), 0.0)
            a = jnp.where(strict, _mm(kbeta[rs], kn[rs], NT_DIMS) * decay, 0.0)
            qk_ref[h, rs, :] = _mm(qn[rs], kn[rs], NT_DIMS) * decay
            t_inv = _unit_lower_inverse(a)
            rhs = jnp.concatenate([vh[rs] * beta[rs], kbeta[rs] * eg[rs]], axis=1)
            uw = _mm(t_inv, rhs)
            us_ref[h, rs, :] = uw[:, :DH]
            for c in range(cpg):
                lo = r * DN_GROUP + c * DN_CHUNK
                wq_ref[h, r * cpg + c, 0:DN_CHUNK, :] = uw[c * DN_CHUNK:(c + 1) * DN_CHUNK, DH:]
                wq_ref[h, r * cpg + c, DN_CHUNK:2 * DN_CHUNK, :] = qe[lo:lo + DN_CHUNK]

    def chunk_step(c, carry):
        r0 = pl.multiple_of(c * DN_CHUNK, DN_CHUNK)
        rows = pl.ds(r0, DN_CHUNK)
        for h in range(HEADS):
            s = s_ref[h]
            m1 = _mm(wq_ref[h, c], s)
            v_new = us_ref[h, rows, :] - m1[:DN_CHUNK]
            oi_ref[h, rows, :] = m1[DN_CHUNK:]
            vn_ref[h, rows, :] = v_new
            s_ref[h] = s * gl_ref[pl.ds(r0, 1), h:h + 1] + _mm(kd_ref[h, rows, :], v_new, TN_DIMS)
        return carry

    lax.fori_loop(0, nc, chunk_step, 0)

    z = z_ref[...]
    dnw = dnw_ref[...]
    for h in range(HEADS):
        parts = []
        for r in range(tm // DN_GROUP):
            rs = slice(r * DN_GROUP, (r + 1) * DN_GROUP)
            parts.append(oi_ref[h, rs, :] + _mm(qk_ref[h, rs, :], vn_ref[h, rs, :]))
        o = jnp.concatenate(parts, axis=0)
        o_ref[:, h * DH:(h + 1) * DH] = _gated_out(o, z[:, h * DH:(h + 1) * DH], dnw).astype(o_ref.dtype)

    @pl.when(i == nt - 1)
    def _():
        sfin_ref[0] = s_ref[...]
        cnew_ref[0] = ext_ref[SUBLANES + tm - hal:SUBLANES + tm, :]


def _deltanet_prompt(conv_in, z, gb, conv_w, dn_norm, batch):
    n, conv_ch = conv_in.shape
    s = n // batch
    tm = min(ROW_TILE, s)
    nt = s // tm
    hal = CONV_W - 1
    hb = tm // SUBLANES
    row = lambda w: pl.BlockSpec((tm, w), lambda b, i: (b * nt + i, 0))
    head_scr = lambda w: pltpu.VMEM((HEADS, tm, w), F32)
    return pl.pallas_call(
        functools.partial(_dn_prompt_kernel, tm=tm),
        grid=(batch, nt),
        in_specs=[row(conv_ch),
                  pl.BlockSpec((SUBLANES, conv_ch), lambda b, i: (jnp.maximum((b * nt + i) * hb - 1, 0), 0)),
                  row(HW), row(LANES), _full((CONV_W, conv_ch)), _full((1, DH))],
        out_specs=[row(HW),
                   pl.BlockSpec((1, HEADS, DH, DH), lambda b, i: (b, 0, 0, 0)),
                   pl.BlockSpec((1, hal, conv_ch), lambda b, i: (b, 0, 0))],
        out_shape=[jax.ShapeDtypeStruct((n, HW), BF16),
                   jax.ShapeDtypeStruct((batch, HEADS, DH, DH), F32),
                   jax.ShapeDtypeStruct((batch, hal, conv_ch), F32)],
        scratch_shapes=[pltpu.VMEM((SUBLANES + tm, conv_ch), F32),
                        pltpu.VMEM((HEADS, DH, DH), F32),
                        pltpu.VMEM((HEADS, tm // DN_CHUNK, 2 * DN_CHUNK, DH), F32),
                        head_scr(DH), head_scr(DH), head_scr(DH), head_scr(DH), head_scr(DN_GROUP),
                        pltpu.VMEM((tm, LANES), F32)],
        compiler_params=_params("arbitrary", "arbitrary"),
        name="deltanet_prompt",
    )(conv_in, conv_in, z, gb, conv_w, dn_norm.reshape(1, DH))


def _dn_sample_kernel(u_ref, cbuf_ref, z_ref, gb_ref, st_ref, cw_ref, dnw_ref,
                      o_ref, snew_ref, cnew_ref, ext_ref, *, t, seqs):
    hal = CONV_W - 1
    cw = cw_ref[...]
    dnw = dnw_ref[...]

    def seq_step(si, carry):
        r0 = pl.multiple_of(si * t, t)
        rows = pl.ds(r0, t)
        ext_ref[SUBLANES - hal:SUBLANES, :] = cbuf_ref[si]
        ext_ref[SUBLANES:SUBLANES + t, :] = u_ref[rows, :]
        y = ext_ref[SUBLANES - hal:SUBLANES - hal + t, :] * cw[0:1]
        for j in range(1, CONV_W):
            y = y + ext_ref[SUBLANES - hal + j:SUBLANES - hal + j + t, :] * cw[j:j + 1]
        cnew_ref[si] = ext_ref[SUBLANES + t - hal:SUBLANES + t, :]
        qkv = y * _sigmoid(y)
        gb = gb_ref[rows, :]
        z = z_ref[rows, :]
        for h in range(HEADS):
            qh = qkv[:, h * DH:(h + 1) * DH]
            kh = qkv[:, HW + h * DH:HW + (h + 1) * DH]
            vh = qkv[:, 2 * HW + h * DH:2 * HW + (h + 1) * DH]
            qn = qh * lax.rsqrt(jnp.sum(qh * qh, axis=-1, keepdims=True) + EPS) * (DH ** -0.5)
            kn = kh * lax.rsqrt(jnp.sum(kh * kh, axis=-1, keepdims=True) + EPS)
            qt = qn.T
            kt = kn.T
            eg = jnp.exp(gb[:, h:h + 1])
            beta = gb[:, HEADS + h:HEADS + h + 1]
            s = st_ref[si, h]
            outs = []
            for ti in range(t):
                kcol = kt[:, ti:ti + 1]
                s = s * eg[ti:ti + 1]
                v_new = beta[ti:ti + 1] * (vh[ti:ti + 1] - jnp.sum(s * kcol, axis=0, keepdims=True))
                s = s + kcol * v_new
                outs.append(jnp.sum(s * qt[:, ti:ti + 1], axis=0, keepdims=True))
            snew_ref[si, h] = s
            o = jnp.concatenate(outs, axis=0)
            o_ref[rows, h * DH:(h + 1) * DH] = _gated_out(o, z[:, h * DH:(h + 1) * DH], dnw)
        return carry

    lax.fori_loop(0, seqs, seq_step, 0)


def _deltanet_sample(conv_in, z, gb, state_conv, state_delta, conv_w, dn_norm):
    n, conv_ch = conv_in.shape
    bd = state_delta.shape[0]
    t = n // bd
    seqs = min(SAMPLE_SEQS, bd)
    hal = CONV_W - 1
    row = lambda w: pl.BlockSpec((seqs * t, w), lambda i: (i, 0))
    st_spec = pl.BlockSpec((seqs, HEADS, DH, DH), lambda i: (i, 0, 0, 0))
    cb_spec = pl.BlockSpec((seqs, hal, conv_ch), lambda i: (i, 0, 0))
    return pl.pallas_call(
        functools.partial(_dn_sample_kernel, t=t, seqs=seqs),
        grid=(bd // seqs,),
        in_specs=[row(conv_ch), cb_spec, row(HW), row(LANES), st_spec, _full((CONV_W, conv_ch)), _full((1, DH))],
        out_specs=[row(HW), st_spec, cb_spec],
        out_shape=[jax.ShapeDtypeStruct((n, HW), F32),
                   jax.ShapeDtypeStruct(state_delta.shape, F32),
                   jax.ShapeDtypeStruct(state_conv.shape, F32)],
        scratch_shapes=[pltpu.VMEM((SUBLANES + t, conv_ch), F32)],
        compiler_params=_params("parallel"),
        name="deltanet_sample",
    )(conv_in, state_conv, z, gb, state_delta, conv_w, dn_norm.reshape(1, DH))


def _moba_prompt_kernel(q_ref, k_ref, vt_ref, km_ref, o_ref, sel_ref, m_ref, l_ref, acc_ref,
                        sa_ref, sb_ref, pa_ref, pb_ref, *, nb):
    qb = pl.program_id(2)
    blk = MOBA_BLOCK
    grp = MOBA_GROUP
    q = q_ref[...]
    qs = (q * (DH ** -0.5 * LOG2_E)).astype(BF16)

    gs = _mm3(km_ref[0], q, NT_DIMS)
    brow = lax.broadcasted_iota(I32, gs.shape, 0)
    gs = jnp.where(brow < qb, gs, NEG_INF)
    sel = jnp.zeros(gs.shape, F32)
    for idx, val in _top_picks(gs, MOBA_TOPK, brow, nb, 0):
        sel = jnp.where((brow == idx) & (val > NEG_INF), 1.0, sel)
    sel_ref[...] = sel

    def scores(row0, rows):
        kj = k_ref[pl.ds(pl.multiple_of(row0, blk), rows), :]
        return lax.dot_general(kj, qs, NT_DIMS, preferred_element_type=F32)

    ki = lax.broadcasted_iota(I32, (blk, blk), 0)
    qi = lax.broadcasted_iota(I32, (blk, blk), 1)
    s = jnp.where(ki <= qi, scores(qb * blk, blk), NEG_INF)
    m = jnp.max(s, axis=0, keepdims=True)
    p = jnp.exp2(s - m)
    m_ref[...] = m
    l_ref[...] = jnp.sum(p, axis=0, keepdims=True)
    acc_ref[...] = _mm(vt_ref[qb], p)

    n_groups = (qb + grp - 1) // grp
    last_group = nb // grp - 1
    sa_ref[...] = scores(0, grp * blk)
    pb_ref[...] = jnp.zeros_like(pb_ref)

    def apply_probs(g, p_ref):
        acc = acc_ref[...]
        for b in range(grp):
            acc = acc + lax.dot_general(vt_ref[g * grp + b], p_ref[b], NN_DIMS, preferred_element_type=F32)
        return acc

    def step(g, s_cur, s_next, p_prev, p_cur):
        s_next[...] = scores(jnp.minimum(g + 1, last_group) * (grp * blk), grp * blk)
        acc = apply_probs(jnp.maximum(g - 1, 0), p_prev)
        j0 = g * grp
        m_old = m_ref[...]
        m_new = m_old
        picked = []
        for b in range(grp):
            on = sel_ref[pl.ds(j0 + b, 1), :] > 0.5
            col_max = jnp.max(s_cur[b * blk:(b + 1) * blk, :], axis=0, keepdims=True)
            m_new = jnp.maximum(m_new, jnp.where(on, col_max, NEG_INF))
            picked.append(on)
        alpha = jnp.exp2(m_old - m_new)
        l = alpha * l_ref[...]
        for b in range(grp):
            p = jnp.exp2(s_cur[b * blk:(b + 1) * blk, :] - jnp.where(picked[b], m_new, -NEG_INF))
            l = l + jnp.sum(p, axis=0, keepdims=True)
            p_cur[b] = p.astype(BF16)
        m_ref[...] = m_new
        l_ref[...] = l
        acc_ref[...] = alpha * acc

    def group_pair(i, carry):
        step(2 * i, sa_ref, sb_ref, pb_ref, pa_ref)
        step(2 * i + 1, sb_ref, sa_ref, pa_ref, pb_ref)
        return carry

    lax.fori_loop(0, n_groups // 2, group_pair, 0)
    odd = n_groups % 2 == 1

    @pl.when(odd)
    def _():
        step(n_groups - 1, sa_ref, sb_ref, pb_ref, pa_ref)
        o_ref[...] = (apply_probs(n_groups - 1, pa_ref) / l_ref[...]).T.astype(o_ref.dtype)

    @pl.when(jnp.logical_not(odd))
    def _():
        o_ref[...] = (apply_probs(jnp.maximum(n_groups - 1, 0), pb_ref) / l_ref[...]).T.astype(o_ref.dtype)


def _moba_prompt(q, kb, vt, kmean, batch):
    n = q.shape[0]
    s = n // batch
    nb = s // MOBA_BLOCK
    assert nb % MOBA_GROUP == 0
    return pl.pallas_call(
        functools.partial(_moba_prompt_kernel, nb=nb),
        grid=(batch, HEADS, nb),
        in_specs=[pl.BlockSpec((MOBA_BLOCK, DH), lambda b, h, j: (b * nb + j, h)),
                  pl.BlockSpec((s, DH), lambda b, h, j: (b, h)),
                  pl.BlockSpec((nb, DH, MOBA_BLOCK), lambda b, h, j: (b, h, 0)),
                  pl.BlockSpec((1, nb, DH), lambda b, h, j: (b, 0, h))],
        out_specs=pl.BlockSpec((MOBA_BLOCK, DH), lambda b, h, j: (b * nb + j, h)),
        out_shape=jax.ShapeDtypeStruct((n, HW), BF16),
        scratch_shapes=[pltpu.VMEM((nb, MOBA_BLOCK), F32), pltpu.VMEM((1, MOBA_BLOCK), F32),
                        pltpu.VMEM((1, MOBA_BLOCK), F32), pltpu.VMEM((DH, MOBA_BLOCK), F32),
                        pltpu.VMEM((MOBA_GROUP * MOBA_BLOCK, MOBA_BLOCK), F32),
                        pltpu.VMEM((MOBA_GROUP * MOBA_BLOCK, MOBA_BLOCK), F32),
                        pltpu.VMEM((MOBA_GROUP, MOBA_BLOCK, MOBA_BLOCK), BF16),
                        pltpu.VMEM((MOBA_GROUP, MOBA_BLOCK, MOBA_BLOCK), BF16)],
        compiler_params=_params("parallel", "parallel", "arbitrary"),
        name="moba_prompt",
    )(q, kb, vt, kmean.reshape(batch, nb, HW))


def _moba_sample_scores_kernel(pt_ref, q_ref, kn_ref, *rest, t, n_pages, page, pps):
    del pt_ref
    k_refs = rest[:pps]
    pp_ref, po_ref, s_ref, ksum_ref = rest[pps:]
    c = pl.program_id(1)
    nc = pl.num_programs(1)
    ppb = MOBA_BLOCK // page
    n_blk = n_pages // ppb
    fold = SUBLANES // HEADS
    scale = DH ** -0.5
    q = q_ref[...]
    qall = jnp.concatenate([q[:, h * DH:(h + 1) * DH] for h in range(HEADS)], axis=0)
    qbf = (qall * scale).astype(BF16)

    for jb in range(pps // ppb):
        ksum = None
        for jp in range(ppb):
            pg = jb * ppb + jp
            kp = k_refs[pg][0]
            s_ref[c * pps + pg] = _mm(qbf, kp, NT_DIMS)
            part = jnp.sum(kp.reshape(page * HEADS // SUBLANES, SUBLANES, DH), axis=0)
            ksum = part if ksum is None else ksum + part
        total = ksum
        for sh in range(1, fold):
            total = total + pltpu.roll(ksum, sh * HEADS, axis=0)
        ksum_ref[pl.ds(pl.multiple_of((c * (pps // ppb) + jb) * SUBLANES, SUBLANES), SUBLANES), :] = total

    @pl.when(c == nc - 1)
    def _():
        ht = HEADS * t
        kmean = ksum_ref[...] * (1.0 / MOBA_BLOCK)
        gate = _mm3(qall, kmean, NT_DIMS)
        grow = lax.broadcasted_iota(I32, gate.shape, 0)
        gcol = lax.broadcasted_iota(I32, gate.shape, 1)
        gate = jnp.where(gcol % SUBLANES == grow // t, gate, NEG_INF)
        picks = [(idx // SUBLANES, val > NEG_INF)
                 for idx, val in _top_picks(gate, MOBA_TOPK, gcol, n_blk * SUBLANES, 1)]

        def block_on(j):
            on = picks[0][1] & (picks[0][0] == j)
            for blk_id, ok in picks[1:]:
                on = on | (ok & (blk_id == j))
            return on

        s_own = _mm(qbf, kn_ref[...], NT_DIMS)
        ro = lax.broadcasted_iota(I32, s_own.shape, 0)
        co = lax.broadcasted_iota(I32, s_own.shape, 1)
        s_own = jnp.where((co % HEADS == ro // t) & (co // HEADS <= ro % t), s_own, NEG_INF)
        rp = lax.broadcasted_iota(I32, (ht, page * HEADS), 0)
        cp = lax.broadcasted_iota(I32, (ht, page * HEADS), 1)
        same_head = cp % HEADS == rp // t
        m = jnp.max(s_own, axis=1, keepdims=True)
        for pg in range(n_pages):
            keep = same_head & block_on(pg // ppb)
            m = jnp.maximum(m, jnp.max(jnp.where(keep, s_ref[pg], NEG_INF), axis=1, keepdims=True))
        p_own = jnp.exp(s_own - m)
        l = jnp.sum(p_own, axis=1, keepdims=True)
        for pg in range(n_pages):
            keep = same_head & block_on(pg // ppb)
            p = jnp.where(keep, jnp.exp(s_ref[pg] - m), 0.0)
            s_ref[pg] = p
            l = l + jnp.sum(p, axis=1, keepdims=True)
        inv = 1.0 / l
        po_ref[0] = p_own * inv
        for pg in range(n_pages):
            pp_ref[0, pg] = (s_ref[pg] * inv).astype(pp_ref.dtype)


def _moba_sample_apply_kernel(pt_ref, pp_ref, po_ref, vn_ref, *rest, t, pps):
    del pt_ref
    v_refs = rest[:pps]
    o_ref, acc_ref = rest[pps:]
    c = pl.program_id(1)
    nc = pl.num_programs(1)

    @pl.when(c == 0)
    def _():
        acc_ref[...] = _mm(po_ref[0], vn_ref[...])

    acc = acc_ref[...]
    for pg in range(pps):
        acc = acc + _mm(pp_ref[0, pg], v_refs[pg][0])
    acc_ref[...] = acc

    @pl.when(c == nc - 1)
    def _():
        for h in range(HEADS):
            o_ref[:, h * DH:(h + 1) * DH] = acc_ref[h * t:(h + 1) * t, :]


def _moba_sample(q, k_new, v_new, cache_k, cache_v, page_table):
    bd, n_pages = page_table.shape
    n = q.shape[0]
    t = n // bd
    prow = cache_k.shape[1]
    page = prow // HEADS
    pps = min(PAGES_PER_STEP, n_pages)
    assert n_pages % pps == 0 and (n_pages * page) % MOBA_BLOCK == 0 and MOBA_BLOCK % page == 0
    assert SUBLANES % HEADS == 0 and pps % (MOBA_BLOCK // page) == 0
    nc = n_pages // pps
    ht = HEADS * t
    n_blk = n_pages * page // MOBA_BLOCK
    pt = page_table.reshape(-1).astype(I32)
    seq_q = pl.BlockSpec((t, HW), lambda b, c, pt: (b, 0))
    seq_kv = pl.BlockSpec((ht, DH), lambda b, c, pt: (b, 0))
    own = pl.BlockSpec((1, ht, ht), lambda b, c, pt: (b, 0, 0))

    def page_spec(i):
        return pl.BlockSpec((1, prow, DH), lambda b, c, pt: (pt[b * n_pages + c * pps + i], 0, 0))

    p_past, p_own = pl.pallas_call(
        functools.partial(_moba_sample_scores_kernel, t=t, n_pages=n_pages, page=page, pps=pps),
        grid_spec=pltpu.PrefetchScalarGridSpec(
            num_scalar_prefetch=1,
            grid=(bd, nc),
            in_specs=[seq_q, seq_kv] + [page_spec(i) for i in range(pps)],
            out_specs=[pl.BlockSpec((1, n_pages, ht, prow), lambda b, c, pt: (b, 0, 0, 0)), own],
            scratch_shapes=[pltpu.VMEM((n_pages, ht, prow), F32),
                            pltpu.VMEM((n_blk * SUBLANES, DH), F32)]),
        out_shape=[jax.ShapeDtypeStruct((bd, n_pages, ht, prow), BF16),
                   jax.ShapeDtypeStruct((bd, ht, ht), F32)],
        compiler_params=_params("parallel", "arbitrary"),
        name="moba_sample_scores",
    )(pt, q, k_new, *([cache_k] * pps))

    return pl.pallas_call(
        functools.partial(_moba_sample_apply_kernel, t=t, pps=pps),
        grid_spec=pltpu.PrefetchScalarGridSpec(
            num_scalar_prefetch=1,
            grid=(bd, nc),
            in_specs=[pl.BlockSpec((1, pps, ht, prow), lambda b, c, pt: (b, c, 0, 0)), own, seq_kv]
                     + [page_spec(i) for i in range(pps)],
            out_specs=seq_q,
            scratch_shapes=[pltpu.VMEM((ht, DH), F32)]),
        out_shape=jax.ShapeDtypeStruct((n, HW), F32),
        compiler_params=_params("parallel", "arbitrary"),
        name="moba_sample_apply",
    )(pt, p_past, p_own, v_new, *([cache_v] * pps))


def _postmix_kernel(dn_ref, mb_ref, x_ref, wo1_ref, wo2_ref, nx_ref, wq_ref, mk_ref, mv_ref, wxo_ref,
                    nf_ref, wr_ref, br_ref, tri_ref,
                    x2_ref, h2_ref, tope_ref, gate_ref, rank_ref, cnt_ref, carry_ref, *, seqs, n_exp, n_mem):
    i = pl.program_id(0)

    @pl.when(i == 0)
    def _():
        carry_ref[...] = jnp.zeros_like(carry_ref)

    x1 = x_ref[...] + _mm(dn_ref[...], wo1_ref[...]) + _mm(mb_ref[...], wo2_ref[...])
    tm = x1.shape[0]
    q = _mm(_rms(x1, nx_ref[...]), wq_ref[...])
    scale = DH ** -0.5
    heads = []
    for h in range(HEADS):
        hs = slice(h * DH, (h + 1) * DH)
        mem_rows = pl.ds(h, n_mem, stride=HEADS)
        if seqs == 1:
            s = _mm(q[:, hs], mk_ref[0, mem_rows, :], NT_DIMS) * scale
            p = jnp.exp(s - jnp.max(s, axis=-1, keepdims=True))
            p = p / jnp.sum(p, axis=-1, keepdims=True)
            heads.append(_mm(p, mv_ref[0, mem_rows, :]))
        else:
            q3 = q[:, hs].reshape(seqs, tm // seqs, DH).astype(BF16)
            mk = jnp.stack([mk_ref[g, mem_rows, :] for g in range(seqs)]).astype(BF16)
            mv = jnp.stack([mv_ref[g, mem_rows, :] for g in range(seqs)]).astype(BF16)
            s = jnp.einsum("gtd,gmd->gtm", q3, mk, preferred_element_type=F32) * scale
            p = jnp.exp(s - jnp.max(s, axis=-1, keepdims=True))
            p = p / jnp.sum(p, axis=-1, keepdims=True)
            o = jnp.einsum("gtm,gmd->gtd", p.astype(BF16), mv, preferred_element_type=F32)
            heads.append(o.reshape(tm, DH))
    x2 = x1 + _mm(jnp.concatenate(heads, axis=1), wxo_ref[...])
    x2_ref[...] = x2
    h2 = _rms(x2, nf_ref[...])
    _store_row_tiles(h2_ref, h2)

    logits = _mm3(h2, wr_ref[...]) + br_ref[...]
    lt = logits.T[:n_exp]
    erow = lax.broadcasted_iota(I32, lt.shape, 0)
    picks = _top_picks(lt, TOP_K, erow, n_exp, 0)
    ex = [jnp.exp(val - picks[0][1]) for _, val in picks]
    den = ex[0]
    for e in ex[1:]:
        den = den + e
    onehot = [erow == idx for idx, _ in picks]
    any_hot = onehot[0]
    for oh in onehot[1:]:
        any_hot = any_hot | oh
    hot = any_hot.astype(F32)
    base = _mm(hot, tri_ref[...]) + carry_ref[:, 0:1]
    for k in range(TOP_K):
        tope_ref[k:k + 1, :] = picks[k][0]
        gate_ref[k:k + 1, :] = ex[k] / den
        rank_ref[k:k + 1, :] = jnp.sum(jnp.where(onehot[k], base, 0.0), axis=0, keepdims=True).astype(I32)
    pad = SUBLANES - TOP_K
    tope_ref[TOP_K:, :] = jnp.zeros((pad, tm), I32)
    gate_ref[TOP_K:, :] = jnp.zeros((pad, tm), F32)
    rank_ref[TOP_K:, :] = jnp.zeros((pad, tm), I32)
    carry = carry_ref[...] + jnp.sum(hot, axis=1, keepdims=True)
    carry_ref[...] = carry
    cnt_ref[...] = carry.astype(I32)


def _post_mix(dn, mb, x, mem_k, mem_v, w_out, norm_x, w_xq, w_xo, norm_ffn, w_router, b_router):
    n, d = x.shape
    n_seq = mem_k.shape[0]
    n_mem = mem_k.shape[1] // HEADS
    rows_per_seq = n // n_seq
    n_exp = w_router.shape[1]
    if rows_per_seq >= ROW_TILE:
        tm, seqs = ROW_TILE, 1
        assert rows_per_seq % tm == 0
        mem_map = lambda i: (i * tm // rows_per_seq, 0, 0)
    else:
        seqs = min(max(SAMPLE_SEQS, LANES // rows_per_seq), n_seq)
        tm = seqs * rows_per_seq
        mem_map = lambda i: (i, 0, 0)
    wo = w_out.astype(BF16)
    wr = jnp.pad(w_router, ((0, 0), (0, LANES - n_exp)))
    br = jnp.pad(b_router, (0, LANES - n_exp)).reshape(1, LANES)
    tri = (jnp.arange(tm)[:, None] < jnp.arange(tm)[None, :]).astype(BF16)
    row = lambda w: pl.BlockSpec((tm, w), lambda i: (i, 0))
    col = pl.BlockSpec((SUBLANES, tm), lambda i: (0, i))
    mem_spec = pl.BlockSpec((seqs, n_mem * HEADS, DH), mem_map)
    return pl.pallas_call(
        functools.partial(_postmix_kernel, seqs=seqs, n_exp=n_exp, n_mem=n_mem),
        grid=(n // tm,),
        in_specs=[row(HW), row(HW), row(d), _full((HW, d)), _full((HW, d)), _full((1, d)), _full((d, HW)),
                  mem_spec, mem_spec, _full((HW, d)), _full((1, d)), _full((d, LANES)), _full((1, LANES)),
                  _full((tm, tm))],
        out_specs=[row(d), pl.BlockSpec((tm * (d // LANES), LANES), lambda i: (i, 0)), col, col, col,
                   _full((n_exp, LANES))],
        out_shape=[jax.ShapeDtypeStruct((n, d), F32), jax.ShapeDtypeStruct((n * (d // LANES), LANES), F32),
                   jax.ShapeDtypeStruct((SUBLANES, n), I32), jax.ShapeDtypeStruct((SUBLANES, n), F32),
                   jax.ShapeDtypeStruct((SUBLANES, n), I32), jax.ShapeDtypeStruct((n_exp, LANES), I32)],
        scratch_shapes=[pltpu.VMEM((n_exp, LANES), F32)],
        compiler_params=_params("arbitrary"),
        name="post_mix",
    )(dn, mb, x, wo[:HW], wo[HW:], norm_x.reshape(1, d), w_xq.astype(BF16), mem_k, mem_v,
      w_xo.astype(BF16), norm_ffn.reshape(1, d), wr, br, tri)


def _store_row_tiles(ref, x):
    rows, width = x.shape
    rt = width // LANES
    for c in range(rt):
        ref[pl.ds(c, rows, stride=rt), :] = x[:, c * LANES:(c + 1) * LANES]


def _load_row_tiles(ref, rows):
    rt = ref.shape[0] // rows
    return jnp.concatenate([ref[pl.ds(c, rows, stride=rt), :] for c in range(rt)], axis=1)


def _plan_kernel(cnt_ref, tope_ref, rank_ref, dest_ref, be_ref, nv_ref, *, n_exp, bm, n_blocks):
    tope = tope_ref[...]
    dest = rank_ref[...]
    start = jnp.int32(0)
    first = pl.program_id(0) == 0
    for e in range(n_exp):
        dest = dest + jnp.where(tope == e, start * bm, 0)
        cnt_e = cnt_ref[e]
        nb_e = (cnt_e + (bm - 1)) // bm

        @pl.when(first)
        def _(e=e, start=start, nb_e=nb_e, cnt_e=cnt_e):
            def fill(j, c):
                be_ref[start + j] = e
                nv_ref[start + j] = jnp.minimum(cnt_e - j * bm, bm)
                return c
            lax.fori_loop(0, nb_e, fill, 0)

        start = start + nb_e
    dest_ref[...] = dest

    @pl.when(first)
    def _():
        def fill(j, c):
            be_ref[j] = n_exp - 1
            nv_ref[j] = 0
            return c
        lax.fori_loop(start, n_blocks, fill, 0)


def _moe_plan(cnt, tope, rank, bm, n_blocks):
    n = tope.shape[1]
    n_exp = cnt.shape[0]
    tl = min(2048, n)
    col = pl.BlockSpec((SUBLANES, tl), lambda i: (0, i))
    smem = lambda: pl.BlockSpec(memory_space=pltpu.SMEM)
    return pl.pallas_call(
        functools.partial(_plan_kernel, n_exp=n_exp, bm=bm, n_blocks=n_blocks),
        grid=(n // tl,),
        in_specs=[smem(), col, col],
        out_specs=[col, smem(), smem()],
        out_shape=[jax.ShapeDtypeStruct((SUBLANES, n), I32), jax.ShapeDtypeStruct((n_blocks,), I32),
                   jax.ShapeDtypeStruct((n_blocks,), I32)],
        compiler_params=_params("arbitrary"),
        name="moe_plan",
    )(cnt[:, 0], tope, rank)


def _tile_copy(src, src_row, dst, dst_row, sem):
    return pltpu.make_async_copy(src.at[pl.ds(src_row * SUBLANES, SUBLANES), :],
                                 dst.at[pl.ds(dst_row * SUBLANES, SUBLANES), :], sem)


def _dispatch_kernel(cnt_ref, dest_ref, x_ref, xs_ref, zero_ref, sem, zsem, *, tm, bm, n_exp, n_blocks):
    @pl.when(pl.program_id(0) == 0)
    def _():
        zero_ref[...] = jnp.zeros_like(zero_ref)

        def block_fill(j):
            return pltpu.make_async_copy(zero_ref, xs_ref.at[pl.ds(j * (bm * SUBLANES), bm * SUBLANES), :], zsem)

        def fills(act):
            start = jnp.int32(0)
            for e in range(n_exp):
                nb_e = (cnt_ref[e] + (bm - 1)) // bm
                start = start + nb_e

                @pl.when(nb_e > 0)
                def _(last=start - 1):
                    act(block_fill(last))

            def tail(j, c):
                act(block_fill(j))
                return c
            lax.fori_loop(start, n_blocks, tail, 0)

        fills(lambda cp: cp.start())
        fills(lambda cp: cp.wait())

    def issue(t, c):
        for k in range(TOP_K):
            _tile_copy(x_ref, t, xs_ref, dest_ref[k, t], sem).start()
        return c

    lax.fori_loop(0, tm, issue, 0, unroll=2)

    def drain(t, c):
        for k in range(TOP_K):
            _tile_copy(x_ref, t, xs_ref, dest_ref[k, t], sem).wait()
        return c

    lax.fori_loop(0, tm, drain, 0)


def _moe_dispatch(h2t, dest, cnt, bm, n_blocks):
    n = h2t.shape[0] // SUBLANES
    tm = min(MOVE_TILE, n)
    n_exp = cnt.shape[0]
    return pl.pallas_call(
        functools.partial(_dispatch_kernel, tm=tm, bm=bm, n_exp=n_exp, n_blocks=n_blocks),
        grid=(n // tm,),
        in_specs=[pl.BlockSpec(memory_space=pltpu.SMEM),
                  pl.BlockSpec((SUBLANES, tm), lambda i: (0, i), memory_space=pltpu.SMEM),
                  pl.BlockSpec((tm * SUBLANES, LANES), lambda i: (i, 0))],
        out_specs=pl.BlockSpec(memory_space=pl.ANY),
        out_shape=jax.ShapeDtypeStruct((n_blocks * bm * SUBLANES, LANES), F32),
        scratch_shapes=[pltpu.VMEM((bm * SUBLANES, LANES), F32), pltpu.SemaphoreType.DMA, pltpu.SemaphoreType.DMA],
        compiler_params=_params("arbitrary"),
        name="moe_dispatch",
    )(cnt[:, 0], dest, h2t)


def _expert_kernel(be_ref, nv_ref, x_ref, wg_ref, bg_ref, wu_ref, bu_ref, wd_ref, bd_ref, y_ref, *, bm):
    del be_ref
    n_valid = nv_ref[pl.program_id(0)]

    @pl.when(n_valid > 0)
    def _():
        x = _load_row_tiles(x_ref, bm).astype(BF16)
        g = jnp.minimum(_mm(x, wg_ref[0]) + bg_ref[0], SWIGLU_LIMIT)
        u = jnp.clip(_mm(x, wu_ref[0]) + bu_ref[0], -SWIGLU_LIMIT, SWIGLU_LIMIT)
        a = g * _sigmoid(SWIGLU_ALPHA * g) * (u + 1.0)
        _store_row_tiles(y_ref, _mm(a, wd_ref[0]) + bd_ref[0])

    @pl.when(n_valid <= 0)
    def _():
        y_ref[...] = jnp.zeros_like(y_ref)


def _moe_experts(xs, block_e, n_valid, w_gate, b_gate, w_up, b_up, w_down, b_down, bm):
    n_exp, d, d_ff = w_gate.shape
    assert d == SUBLANES * LANES, "a model row must be exactly one (SUBLANES, LANES) tile"
    n_blocks = xs.shape[0] // (bm * SUBLANES)
    wspec = lambda r, c: pl.BlockSpec((1, r, c), lambda i, be, nv: (be[i], 0, 0))
    rows = pl.BlockSpec((bm * SUBLANES, LANES), lambda i, be, nv: (i, 0))
    return pl.pallas_call(
        functools.partial(_expert_kernel, bm=bm),
        grid_spec=pltpu.PrefetchScalarGridSpec(
            num_scalar_prefetch=2,
            grid=(n_blocks,),
            in_specs=[rows, wspec(d, d_ff), wspec(1, d_ff), wspec(d, d_ff), wspec(1, d_ff), wspec(d_ff, d), wspec(1, d)],
            out_specs=rows),
        out_shape=jax.ShapeDtypeStruct(xs.shape, F32),
        compiler_params=_params("arbitrary"),
        name="moe_experts",
    )(block_e, n_valid, xs, w_gate.astype(BF16), b_gate.reshape(n_exp, 1, d_ff), w_up.astype(BF16),
      b_up.reshape(n_exp, 1, d_ff), w_down.astype(BF16), b_down.reshape(n_exp, 1, d))


def _combine_kernel(dest_ref, dnext_ref, gate_ref, x_ref, nw_ref, ys_ref, o_ref, buf_a, buf_b, sem_a, sem_b, *, tm):
    i = pl.program_id(0)
    n = pl.num_programs(0)

    def gather(d_ref, buf, sem, act):
        def body(t, c):
            for k in range(TOP_K):
                act(_tile_copy(ys_ref, d_ref[k, t], buf.at[k], t, sem))
            return c
        return body

    def start(d_ref, buf, sem):
        lax.fori_loop(0, tm, gather(d_ref, buf, sem, lambda cp: cp.start()), 0, unroll=2)

    def step(cur_buf, cur_sem, nxt_buf, nxt_sem):
        @pl.when(i + 1 < n)
        def _():
            start(dnext_ref, nxt_buf, nxt_sem)

        lax.fori_loop(0, tm, gather(dest_ref, cur_buf, cur_sem, lambda cp: cp.wait()), 0)
        gt = gate_ref[...].T
        acc = x_ref[...]
        for k in range(TOP_K):
            acc = acc + gt[:, k:k + 1] * _load_row_tiles(cur_buf.at[k], tm)
        o_ref[...] = _rms(acc, nw_ref[...])

    @pl.when(i == 0)
    def _():
        start(dest_ref, buf_a, sem_a)

    @pl.when(i % 2 == 0)
    def _():
        step(buf_a, sem_a, buf_b, sem_b)

    @pl.when(i % 2 == 1)
    def _():
        step(buf_b, sem_b, buf_a, sem_a)


def _moe_combine(ys, dest, gates, x2, norm_final):
    n, d = x2.shape
    tm = min(MOVE_TILE, n)
    nt = n // tm
    col = lambda ms: pl.BlockSpec((SUBLANES, tm), lambda i: (0, i), memory_space=ms)
    nxt = pl.BlockSpec((SUBLANES, tm), lambda i: (0, jnp.minimum(i + 1, nt - 1)), memory_space=pltpu.SMEM)
    buf = pltpu.VMEM((TOP_K, tm * SUBLANES, LANES), F32)
    return pl.pallas_call(
        functools.partial(_combine_kernel, tm=tm),
        grid=(nt,),
        in_specs=[col(pltpu.SMEM), nxt, col(pltpu.VMEM), pl.BlockSpec((tm, d), lambda i: (i, 0)), _full((1, d)),
                  pl.BlockSpec(memory_space=pl.ANY)],
        out_specs=pl.BlockSpec((tm, d), lambda i: (i, 0)),
        out_shape=jax.ShapeDtypeStruct((n, d), F32),
        scratch_shapes=[buf, buf, pltpu.SemaphoreType.DMA, pltpu.SemaphoreType.DMA],
        compiler_params=_params("arbitrary"),
        name="moe_combine",
    )(dest, dest, gates, x2, norm_final.reshape(1, d), ys)


def _moe_and_final_norm(x2, h2t, tope, gates, rank, cnt, w_gate, b_gate, w_up, b_up, w_down, b_down, norm_final):
    n = x2.shape[0]
    n_exp = w_gate.shape[0]
    bm = MOE_BLOCK_ROWS
    n_blocks = -(-(n * TOP_K + n_exp * (bm - 1)) // bm)
    dest, block_e, n_valid = _moe_plan(cnt, tope, rank, bm, n_blocks)
    xs = _moe_dispatch(h2t, dest, cnt, bm, n_blocks)
    ys = _moe_experts(xs, block_e, n_valid, w_gate, b_gate, w_up, b_up, w_down, b_down, bm)
    return _moe_combine(ys, dest, gates, x2, norm_final)


def kernel(x_prompt, x_sample, mem_prompt, cache_k, cache_v, page_table, state_delta, state_conv, cache_mem_k, cache_mem_v, norm_mix, w_in, conv_w, a_log, dt_bias, dn_norm, w_out, norm_x, norm_mem, w_xq, w_xk, w_xv, w_xo, norm_ffn, w_router, b_router, w_gate, b_gate, w_up, b_up, w_down, b_down, norm_final):
    depth = w_in.shape[0]
    assert depth == 1, "single-layer stack"
    b, s, d = x_prompt.shape
    bd, t, _ = x_sample.shape
    n_mem = mem_prompt.shape[1]
    n_pool, page = cache_k.shape[1], cache_k.shape[2]
    past = page_table.shape[1] * page
    assert past % MOBA_BLOCK == 0, "past length must be whole MoBA blocks"
    l = 0
    moe_w = (w_gate[l], b_gate[l], w_up[l], b_up[l], w_down[l], b_down[l])

    xs = x_sample.reshape(bd * t, d)
    tile_s = min(ROW_TILE, bd * t)
    cos_s, sin_s = _rope_tables(past + jnp.arange(tile_s) % t)
    conv_s, z_s, gb_s, q_s, k4_s, v4_s = _in_proj(xs, norm_mix[l], w_in[l], a_log[l], dt_bias[l],
                                                  cos_s, sin_s, prompt=False)
    dn_s, s_s, cv_s = _deltanet_sample(conv_s, z_s, gb_s, state_conv[l], state_delta[l], conv_w[l], dn_norm[l])
    mb_s = _moba_sample(q_s, k4_s, v4_s, cache_k.reshape(n_pool, page * HEADS, DH),
                        cache_v.reshape(n_pool, page * HEADS, DH), page_table)
    x2s, h2s, tope_s, gates_s, rank_s, cnt_s = _post_mix(
        dn_s, mb_s, xs, cache_mem_k.reshape(bd, n_mem * HEADS, DH), cache_mem_v.reshape(bd, n_mem * HEADS, DH),
        w_out[l], norm_x[l], w_xq[l], w_xo[l], norm_ffn[l], w_router[l], b_router[l])
    y_sample = _moe_and_final_norm(x2s, h2s, tope_s, gates_s, rank_s, cnt_s, *moe_w, norm_final)

    xp = x_prompt.reshape(b * s, d)
    cos_p, sin_p = _rope_tables(jnp.arange(s))
    conv_in, z, gb, q, k, v, kb, vt, kmean = _in_proj(xp, norm_mix[l], w_in[l], a_log[l], dt_bias[l],
                                                      cos_p, sin_p, prompt=True)
    dn, s_p, cv_p = _deltanet_prompt(conv_in, z, gb, conv_w[l], dn_norm[l], b)
    mb = _moba_prompt(q, kb, vt, kmean, b)
    mk_p, mv_p = _mem_kv(mem_prompt.reshape(b * n_mem, d), norm_mem[l], w_xk[l], w_xv[l])
    x2, h2, tope, gates, rank, cnt = _post_mix(
        dn, mb, xp, mk_p.reshape(b, n_mem * HEADS, DH), mv_p.reshape(b, n_mem * HEADS, DH), w_out[l], norm_x[l],
        w_xq[l], w_xo[l], norm_ffn[l], w_router[l], b_router[l])
    y_prompt = _moe_and_final_norm(x2, h2, tope, gates, rank, cnt, *moe_w, norm_final)

    hd = (HEADS, DH)
    return (y_prompt.reshape(b, s, d), y_sample.reshape(bd, t, d),
            k.reshape(1, b, s, *hd), v.reshape(1, b, s, *hd),
            k4_s.reshape(1, bd, t, *hd), v4_s.reshape(1, bd, t, *hd),
            s_p[None], s_s[None], cv_p[None], cv_s[None],
            mk_p.reshape(1, b, n_mem, *hd), mv_p.reshape(1, b, n_mem, *hd))
```

```python
import functools
import math

import jax
import jax.numpy as jnp
from jax import lax
from jax.experimental import pallas as pl
from jax.experimental.pallas import tpu as pltpu

F32 = jnp.float32
BF16 = jnp.bfloat16
I32 = jnp.int32

HEADS = 4
DH = 128
HW = HEADS * DH
CONV_W = 4
DN_CHUNK = 64
MOBA_BLOCK = 256
MOBA_TOPK = 3
ROPE_THETA = 10000.0
TOP_K = 4
SWIGLU_ALPHA = 1.702
SWIGLU_LIMIT = 7.0
EPS = 1e-6
NEG_INF = float("-inf")
LOG2_E = math.log2(math.e)

LANES = 128
SUBLANES = 8
VMEM_LIMIT = 56 * 1024 * 1024

ROW_TILE = 512
DN_GROUP = 256
MOBA_GROUP = 4
MOE_BLOCK_ROWS = 512
MOVE_TILE = 256
PAGES_PER_STEP = 16
SAMPLE_SEQS = 8

NN_DIMS = (((1,), (0,)), ((), ()))
NT_DIMS = (((1,), (1,)), ((), ()))
TN_DIMS = (((0,), (0,)), ((), ()))


def _mm(a, b, dims=NN_DIMS):
    return lax.dot_general(a.astype(BF16), b.astype(BF16), dims, preferred_element_type=F32)


def _split(a):
    hi = a.astype(BF16)
    lo = (a - hi.astype(F32)).astype(BF16)
    return hi, lo


def _mm3(a, b, dims=NN_DIMS):
    a_hi, a_lo = _split(a)
    b_hi, b_lo = _split(b)
    d = functools.partial(lax.dot_general, dimension_numbers=dims, preferred_element_type=F32)
    return d(a_hi, b_hi) + (d(a_hi, b_lo) + d(a_lo, b_hi))


def _rms(x, w):
    return x * lax.rsqrt(jnp.mean(x * x, axis=-1, keepdims=True) + EPS) * w


def _sigmoid(x):
    return 1.0 / (1.0 + jnp.exp(-x))


def _softplus(x):
    return jnp.maximum(x, 0.0) + jnp.log1p(jnp.exp(-jnp.abs(x)))


def _params(*sem):
    return pltpu.CompilerParams(dimension_semantics=sem, vmem_limit_bytes=VMEM_LIMIT)


def _full(shape):
    nd = len(shape)
    return pl.BlockSpec(shape, lambda *_: (0,) * nd)


def _top_picks(scores, n_pick, ids, n_ids, axis):
    picks = []
    cur = scores
    for _ in range(n_pick):
        m = jnp.max(cur, axis=axis, keepdims=True)
        idx = jnp.min(jnp.where(cur == m, ids, n_ids), axis=axis, keepdims=True)
        picks.append((idx, m))
        cur = jnp.where(ids == idx, NEG_INF, cur)
    return picks


def _memkv_kernel(m_ref, nw_ref, wk_ref, wv_ref, k_ref, v_ref):
    h = _rms(m_ref[...], nw_ref[...])
    rows = h.shape[0]
    k = _mm(h, wk_ref[...])
    v = _mm(h, wv_ref[...])
    for hd in range(HEADS):
        k_ref[pl.ds(hd, rows, stride=HEADS), :] = k[:, hd * DH:(hd + 1) * DH]
        v_ref[pl.ds(hd, rows, stride=HEADS), :] = v[:, hd * DH:(hd + 1) * DH]


def _mem_kv(mem, norm_mem, w_xk, w_xv):
    n, d = mem.shape
    tm = min(ROW_TILE, n)
    return pl.pallas_call(
        _memkv_kernel,
        grid=(n // tm,),
        in_specs=[pl.BlockSpec((tm, d), lambda i: (i, 0)), _full((1, d)), _full(w_xk.shape), _full(w_xv.shape)],
        out_specs=[pl.BlockSpec((tm * HEADS, DH), lambda i: (i, 0))] * 2,
        out_shape=[jax.ShapeDtypeStruct((n * HEADS, DH), F32)] * 2,
        compiler_params=_params("parallel"),
        name="mem_kv",
    )(mem, norm_mem.reshape(1, d), w_xk.astype(BF16), w_xv.astype(BF16))


def _rope(x, cos, sin):
    return x * cos + pltpu.roll(x, DH // 2, axis=1) * sin


def _inproj_kernel(x_ref, nw_ref, w_ref, wab_ref, cos_ref, sin_ref, alog_ref, dtb_ref,
                   conv_ref, z_ref, gb_ref, q_ref, k_ref, v_ref, *rest, conv_ch, prompt):
    x = x_ref[...]
    hb = _rms(x, nw_ref[...]).astype(BF16)
    conv_ref[...] = _mm(hb, w_ref[:, 0:conv_ch])
    z_ref[...] = _mm(hb, w_ref[:, conv_ch:conv_ch + HW])
    ab = _mm(hb, wab_ref[...])
    lane = lax.broadcasted_iota(I32, ab.shape, 1)
    g = -jnp.exp(alog_ref[...]) * _softplus(ab + dtb_ref[...])
    gb_ref[...] = jnp.where(lane < HEADS, g, _sigmoid(ab))
    off = conv_ch + HW
    cos = cos_ref[...]
    sin = sin_ref[...]
    q = _mm(hb, w_ref[:, off:off + HW])
    k = _mm(hb, w_ref[:, off + HW:off + 2 * HW])
    v = _mm(hb, w_ref[:, off + 2 * HW:off + 3 * HW])
    rows = x.shape[0]
    k_rot = []
    for h in range(HEADS):
        sl = slice(h * DH, (h + 1) * DH)
        q_ref[:, sl] = _rope(q[:, sl], cos, sin)
        k_rot.append(_rope(k[:, sl], cos, sin))
        k_ref[pl.ds(h, rows, stride=HEADS), :] = k_rot[h]
        v_ref[pl.ds(h, rows, stride=HEADS), :] = v[:, sl]
    if prompt:
        kb_ref, vt_ref, kmean_ref = rest
        for h in range(HEADS):
            kb_ref[:, h * DH:(h + 1) * DH] = k_rot[h].astype(BF16)
        for blk in range(x.shape[0] // MOBA_BLOCK):
            rows = slice(blk * MOBA_BLOCK, (blk + 1) * MOBA_BLOCK)
            vt_ref[blk] = v[rows].T.astype(BF16)
            for h in range(HEADS):
                kmean_ref[blk, :, h * DH:(h + 1) * DH] = jnp.mean(k_rot[h][rows], axis=0, keepdims=True)


def _in_proj(x, norm_w, w_in, a_log, dt_bias, cos, sin, *, prompt):
    n, d = x.shape
    conv_ch = 3 * HW
    tm = min(ROW_TILE, n)
    off_a = conv_ch + HW
    w_main = jnp.concatenate([w_in[:, :off_a], w_in[:, off_a + 2 * HEADS:]], axis=1).astype(BF16)
    w_ab = jnp.pad(w_in[:, off_a:off_a + 2 * HEADS], ((0, 0), (0, LANES - 2 * HEADS))).astype(BF16)
    alog = jnp.pad(a_log, (0, LANES - HEADS)).reshape(1, LANES)
    dtb = jnp.pad(dt_bias, (0, LANES - HEADS)).reshape(1, LANES)
    n_pos = cos.shape[0] // tm
    row = lambda w: pl.BlockSpec((tm, w), lambda i: (i, 0))
    kv_spec = pl.BlockSpec((tm * HEADS, DH), lambda i: (i, 0))
    out_specs = [row(conv_ch), row(HW), row(LANES), row(HW), kv_spec, kv_spec]
    out_shape = ([jax.ShapeDtypeStruct((n, w), F32) for w in (conv_ch, HW, LANES, HW)]
                 + [jax.ShapeDtypeStruct((n * HEADS, DH), F32)] * 2)
    if prompt:
        nb = tm // MOBA_BLOCK
        out_specs += [row(HW),
                      pl.BlockSpec((nb, HW, MOBA_BLOCK), lambda i: (i, 0, 0)),
                      pl.BlockSpec((nb, 1, HW), lambda i: (i, 0, 0))]
        out_shape += [jax.ShapeDtypeStruct((n, HW), BF16),
                      jax.ShapeDtypeStruct((n // MOBA_BLOCK, HW, MOBA_BLOCK), BF16),
                      jax.ShapeDtypeStruct((n // MOBA_BLOCK, 1, HW), F32)]
    return pl.pallas_call(
        functools.partial(_inproj_kernel, conv_ch=conv_ch, prompt=prompt),
        grid=(n // tm,),
        in_specs=[row(d), _full((1, d)), _full(w_main.shape), _full(w_ab.shape),
                  pl.BlockSpec((tm, DH), lambda i: (i % n_pos, 0)),
                  pl.BlockSpec((tm, DH), lambda i: (i % n_pos, 0)),
                  _full((1, LANES)), _full((1, LANES))],
        out_specs=out_specs,
        out_shape=out_shape,
        compiler_params=_params("parallel"),
        name="in_proj",
    )(x, norm_w.reshape(1, d), w_main, w_ab, cos, sin, alog, dtb)


def _rope_tables(pos):
    half = DH // 2
    inv = ROPE_THETA ** (-jnp.arange(half, dtype=F32) / half)
    ang = pos.astype(F32)[:, None] * inv[None, :]
    c, s = jnp.cos(ang), jnp.sin(ang)
    return jnp.concatenate([c, c], axis=1), jnp.concatenate([-s, s], axis=1)


def _chunk_cumsum(g, row_in_chunk):
    for sh in (1, 2, 4, 8, 16, 32):
        g = g + jnp.where(row_in_chunk >= sh, pltpu.roll(g, sh, axis=0), 0.0)
    return g


def _unit_lower_inverse(a):
    n = a.shape[0]
    eye = (lax.broadcasted_iota(I32, (n, n), 0) == lax.broadcasted_iota(I32, (n, n), 1)).astype(F32)
    p = eye - a
    ak = a
    for _ in range(5):
        ak = _mm(ak, ak)
        p = p + _mm(p, ak)
    return p


def _gated_out(o, z, dn_w):
    return _rms(o, dn_w) * (z * _sigmoid(z))


def _dn_prompt_kernel(u_ref, halo_ref, z_ref, gb_ref, cw_ref, dnw_ref,
                      o_ref, sfin_ref, cnew_ref,
                      ext_ref, s_ref, wq_ref, us_ref, kd_ref, vn_ref, oi_ref, qk_ref, gl_ref, *, tm):
    i = pl.program_id(1)
    nt = pl.num_programs(1)
    nc = tm // DN_CHUNK
    cpg = DN_GROUP // DN_CHUNK
    hal = CONV_W - 1

    @pl.when(i == 0)
    def _():
        s_ref[...] = jnp.zeros_like(s_ref)

    u = u_ref[...]
    ext_ref[0:SUBLANES, :] = jnp.where(i > 0, halo_ref[...], 0.0)
    ext_ref[SUBLANES:SUBLANES + tm, :] = u
    cw = cw_ref[...]
    y = u * cw[hal:hal + 1]
    for j in range(hal):
        y = y + ext_ref[SUBLANES - hal + j:SUBLANES - hal + j + tm, :] * cw[j:j + 1]
    qkv = y * _sigmoid(y)

    gb = gb_ref[...]
    row_in_chunk = lax.broadcasted_iota(I32, gb.shape, 0) % DN_CHUNK
    gc = _chunk_cumsum(gb, row_in_chunk)
    gc3 = gc.reshape(nc, DN_CHUNK, LANES)
    glast = jnp.broadcast_to(gc3[:, DN_CHUNK - 1:DN_CHUNK, :], gc3.shape).reshape(tm, LANES)
    gl_ref[...] = jnp.exp(glast)
    gct = gc.T

    gi = lax.broadcasted_iota(I32, (DN_GROUP, DN_GROUP), 0)
    gj = lax.broadcasted_iota(I32, (DN_GROUP, DN_GROUP), 1)
    same = (gi // DN_CHUNK) == (gj // DN_CHUNK)
    causal = same & (gi >= gj)
    strict = same & (gi > gj)

    for h in range(HEADS):
        qh = qkv[:, h * DH:(h + 1) * DH]
        kh = qkv[:, HW + h * DH:HW + (h + 1) * DH]
        vh = qkv[:, 2 * HW + h * DH:2 * HW + (h + 1) * DH]
        qn = qh * lax.rsqrt(jnp.sum(qh * qh, axis=-1, keepdims=True) + EPS) * (DH ** -0.5)
        kn = kh * lax.rsqrt(jnp.sum(kh * kh, axis=-1, keepdims=True) + EPS)
        gcol = gc[:, h:h + 1]
        beta = gb[:, HEADS + h:HEADS + h + 1]
        eg = jnp.exp(gcol)
        kbeta = kn * beta
        kd_ref[h] = kn * jnp.exp(glast[:, h:h + 1] - gcol)
        qe = qn * eg
        for r in range(tm // DN_GROUP):
            rs = slice(r * DN_GROUP, (r + 1) * DN_GROUP)
            decay = jnp.where(causal, jnp.exp(jnp.minimum(gcol[rs] - gct[h:h + 1, rs], 0.0)), 0.0)
            a = jnp.where(strict, _mm(kbeta[rs], kn[rs], NT_DIMS) * decay, 0.0)
            qk_ref[h, rs, :] = _mm(qn[rs], kn[rs], NT_DIMS) * decay
            t_inv = _unit_lower_inverse(a)
            rhs = jnp.concatenate([vh[rs] * beta[rs], kbeta[rs] * eg[rs]], axis=1)
            uw = _mm(t_inv, rhs)
            us_ref[h, rs, :] = uw[:, :DH]
            for c in range(cpg):
                lo = r * DN_GROUP + c * DN_CHUNK
                wq_ref[h, r * cpg + c, 0:DN_CHUNK, :] = uw[c * DN_CHUNK:(c + 1) * DN_CHUNK, DH:]
                wq_ref[h, r * cpg + c, DN_CHUNK:2 * DN_CHUNK, :] = qe[lo:lo + DN_CHUNK]

    def chunk_step(c, carry):
        r0 = pl.multiple_of(c * DN_CHUNK, DN_CHUNK)
        rows = pl.ds(r0, DN_CHUNK)
        for h in range(HEADS):
            s = s_ref[h]
            m1 = _mm(wq_ref[h, c], s)
            v_new = us_ref[h, rows, :] - m1[:DN_CHUNK]
            oi_ref[h, rows, :] = m1[DN_CHUNK:]
            vn_ref[h, rows, :] = v_new
            s_ref[h] = s * gl_ref[pl.ds(r0, 1), h:h + 1] + _mm(kd_ref[h, rows, :], v_new, TN_DIMS)
        return carry

    lax.fori_loop(0, nc, chunk_step, 0)

    z = z_ref[...]
    dnw = dnw_ref[...]
    for h in range(HEADS):
        parts = []
        for r in range(tm // DN_GROUP):
            rs = slice(r * DN_GROUP, (r + 1) * DN_GROUP)
            parts.append(oi_ref[h, rs, :] + _mm(qk_ref[h, rs, :], vn_ref[h, rs, :]))
        o = jnp.concatenate(parts, axis=0)
        o_ref[:, h * DH:(h + 1) * DH] = _gated_out(o, z[:, h * DH:(h + 1) * DH], dnw).astype(o_ref.dtype)

    @pl.when(i == nt - 1)
    def _():
        sfin_ref[0] = s_ref[...]
        cnew_ref[0] = ext_ref[SUBLANES + tm - hal:SUBLANES + tm, :]


def _deltanet_prompt(conv_in, z, gb, conv_w, dn_norm, batch):
    n, conv_ch = conv_in.shape
    s = n // batch
    tm = min(ROW_TILE, s)
    nt = s // tm
    hal = CONV_W - 1
    hb = tm // SUBLANES
    row = lambda w: pl.BlockSpec((tm, w), lambda b, i: (b * nt + i, 0))
    head_scr = lambda w: pltpu.VMEM((HEADS, tm, w), F32)
    return pl.pallas_call(
        functools.partial(_dn_prompt_kernel, tm=tm),
        grid=(batch, nt),
        in_specs=[row(conv_ch),
                  pl.BlockSpec((SUBLANES, conv_ch), lambda b, i: (jnp.maximum((b * nt + i) * hb - 1, 0), 0)),
                  row(HW), row(LANES), _full((CONV_W, conv_ch)), _full((1, DH))],
        out_specs=[row(HW),
                   pl.BlockSpec((1, HEADS, DH, DH), lambda b, i: (b, 0, 0, 0)),
                   pl.BlockSpec((1, hal, conv_ch), lambda b, i: (b, 0, 0))],
        out_shape=[jax.ShapeDtypeStruct((n, HW), BF16),
                   jax.ShapeDtypeStruct((batch, HEADS, DH, DH), F32),
                   jax.ShapeDtypeStruct((batch, hal, conv_ch), F32)],
        scratch_shapes=[pltpu.VMEM((SUBLANES + tm, conv_ch), F32),
                        pltpu.VMEM((HEADS, DH, DH), F32),
                        pltpu.VMEM((HEADS, tm // DN_CHUNK, 2 * DN_CHUNK, DH), F32),
                        head_scr(DH), head_scr(DH), head_scr(DH), head_scr(DH), head_scr(DN_GROUP),
                        pltpu.VMEM((tm, LANES), F32)],
        compiler_params=_params("arbitrary", "arbitrary"),
        name="deltanet_prompt",
    )(conv_in, conv_in, z, gb, conv_w, dn_norm.reshape(1, DH))


def _dn_sample_kernel(u_ref, cbuf_ref, z_ref, gb_ref, st_ref, cw_ref, dnw_ref,
                      o_ref, snew_ref, cnew_ref, ext_ref, *, t, seqs):
    hal = CONV_W - 1
    cw = cw_ref[...]
    dnw = dnw_ref[...]

    def seq_step(si, carry):
        r0 = pl.multiple_of(si * t, t)
        rows = pl.ds(r0, t)
        ext_ref[SUBLANES - hal:SUBLANES, :] = cbuf_ref[si]
        ext_ref[SUBLANES:SUBLANES + t, :] = u_ref[rows, :]
        y = ext_ref[SUBLANES - hal:SUBLANES - hal + t, :] * cw[0:1]
        for j in range(1, CONV_W):
            y = y + ext_ref[SUBLANES - hal + j:SUBLANES - hal + j + t, :] * cw[j:j + 1]
        cnew_ref[si] = ext_ref[SUBLANES + t - hal:SUBLANES + t, :]
        qkv = y * _sigmoid(y)
        gb = gb_ref[rows, :]
        z = z_ref[rows, :]
        for h in range(HEADS):
            qh = qkv[:, h * DH:(h + 1) * DH]
            kh = qkv[:, HW + h * DH:HW + (h + 1) * DH]
            vh = qkv[:, 2 * HW + h * DH:2 * HW + (h + 1) * DH]
            qn = qh * lax.rsqrt(jnp.sum(qh * qh, axis=-1, keepdims=True) + EPS) * (DH ** -0.5)
            kn = kh * lax.rsqrt(jnp.sum(kh * kh, axis=-1, keepdims=True) + EPS)
            qt = qn.T
            kt = kn.T
            eg = jnp.exp(gb[:, h:h + 1])
            beta = gb[:, HEADS + h:HEADS + h + 1]
            s = st_ref[si, h]
            outs = []
            for ti in range(t):
                kcol = kt[:, ti:ti + 1]
                s = s * eg[ti:ti + 1]
                v_new = beta[ti:ti + 1] * (vh[ti:ti + 1] - jnp.sum(s * kcol, axis=0, keepdims=True))
                s = s + kcol * v_new
                outs.append(jnp.sum(s * qt[:, ti:ti + 1], axis=0, keepdims=True))
            snew_ref[si, h] = s
            o = jnp.concatenate(outs, axis=0)
            o_ref[rows, h * DH:(h + 1) * DH] = _gated_out(o, z[:, h * DH:(h + 1) * DH], dnw)
        return carry

    lax.fori_loop(0, seqs, seq_step, 0)


def _deltanet_sample(conv_in, z, gb, state_conv, state_delta, conv_w, dn_norm):
    n, conv_ch = conv_in.shape
    bd = state_delta.shape[0]
    t = n // bd
    seqs = min(SAMPLE_SEQS, bd)
    hal = CONV_W - 1
    row = lambda w: pl.BlockSpec((seqs * t, w), lambda i: (i, 0))
    st_spec = pl.BlockSpec((seqs, HEADS, DH, DH), lambda i: (i, 0, 0, 0))
    cb_spec = pl.BlockSpec((seqs, hal, conv_ch), lambda i: (i, 0, 0))
    return pl.pallas_call(
        functools.partial(_dn_sample_kernel, t=t, seqs=seqs),
        grid=(bd // seqs,),
        in_specs=[row(conv_ch), cb_spec, row(HW), row(LANES), st_spec, _full((CONV_W, conv_ch)), _full((1, DH))],
        out_specs=[row(HW), st_spec, cb_spec],
        out_shape=[jax.ShapeDtypeStruct((n, HW), F32),
                   jax.ShapeDtypeStruct(state_delta.shape, F32),
                   jax.ShapeDtypeStruct(state_conv.shape, F32)],
        scratch_shapes=[pltpu.VMEM((SUBLANES + t, conv_ch), F32)],
        compiler_params=_params("parallel"),
        name="deltanet_sample",
    )(conv_in, state_conv, z, gb, state_delta, conv_w, dn_norm.reshape(1, DH))


def _moba_prompt_kernel(q_ref, k_ref, vt_ref, km_ref, o_ref, sel_ref, m_ref, l_ref, acc_ref,
                        sa_ref, sb_ref, pa_ref, pb_ref, *, nb):
    qb = pl.program_id(2)
    blk = MOBA_BLOCK
    grp = MOBA_GROUP
    q = q_ref[...]
    qs = (q * (DH ** -0.5 * LOG2_E)).astype(BF16)

    gs = _mm3(km_ref[0], q, NT_DIMS)
    brow = lax.broadcasted_iota(I32, gs.shape, 0)
    gs = jnp.where(brow < qb, gs, NEG_INF)
    sel = jnp.zeros(gs.shape, F32)
    for idx, val in _top_picks(gs, MOBA_TOPK, brow, nb, 0):
        sel = jnp.where((brow == idx) & (val > NEG_INF), 1.0, sel)
    sel_ref[...] = sel

    def scores(row0, rows):
        kj = k_ref[pl.ds(pl.multiple_of(row0, blk), rows), :]
        return lax.dot_general(kj, qs, NT_DIMS, preferred_element_type=F32)

    ki = lax.broadcasted_iota(I32, (blk, blk), 0)
    qi = lax.broadcasted_iota(I32, (blk, blk), 1)
    s = jnp.where(ki <= qi, scores(qb * blk, blk), NEG_INF)
    m = jnp.max(s, axis=0, keepdims=True)
    p = jnp.exp2(s - m)
    m_ref[...] = m
    l_ref[...] = jnp.sum(p, axis=0, keepdims=True)
    acc_ref[...] = _mm(vt_ref[qb], p)

    n_groups = (qb + grp - 1) // grp
    last_group = nb // grp - 1
    sa_ref[...] = scores(0, grp * blk)
    pb_ref[...] = jnp.zeros_like(pb_ref)

    def apply_probs(g, p_ref):
        acc = acc_ref[...]
        for b in range(grp):
            acc = acc + lax.dot_general(vt_ref[g * grp + b], p_ref[b], NN_DIMS, preferred_element_type=F32)
        return acc

    def step(g, s_cur, s_next, p_prev, p_cur):
        s_next[...] = scores(jnp.minimum(g + 1, last_group) * (grp * blk), grp * blk)
        acc = apply_probs(jnp.maximum(g - 1, 0), p_prev)
        j0 = g * grp
        m_old = m_ref[...]
        m_new = m_old
        picked = []
        for b in range(grp):
            on = sel_ref[pl.ds(j0 + b, 1), :] > 0.5
            col_max = jnp.max(s_cur[b * blk:(b + 1) * blk, :], axis=0, keepdims=True)
            m_new = jnp.maximum(m_new, jnp.where(on, col_max, NEG_INF))
            picked.append(on)
        alpha = jnp.exp2(m_old - m_new)
        l = alpha * l_ref[...]
        for b in range(grp):
            p = jnp.exp2(s_cur[b * blk:(b + 1) * blk, :] - jnp.where(picked[b], m_new, -NEG_INF))
            l = l + jnp.sum(p, axis=0, keepdims=True)
            p_cur[b] = p.astype(BF16)
        m_ref[...] = m_new
        l_ref[...] = l
        acc_ref[...] = alpha * acc

    def group_pair(i, carry):
        step(2 * i, sa_ref, sb_ref, pb_ref, pa_ref)
        step(2 * i + 1, sb_ref, sa_ref, pa_ref, pb_ref)
        return carry

    lax.fori_loop(0, n_groups // 2, group_pair, 0)
    odd = n_groups % 2 == 1

    @pl.when(odd)
    def _():
        step(n_groups - 1, sa_ref, sb_ref, pb_ref, pa_ref)
        o_ref[...] = (apply_probs(n_groups - 1, pa_ref) / l_ref[...]).T.astype(o_ref.dtype)

    @pl.when(jnp.logical_not(odd))
    def _():
        o_ref[...] = (apply_probs(jnp.maximum(n_groups - 1, 0), pb_ref) / l_ref[...]).T.astype(o_ref.dtype)


def _moba_prompt(q, kb, vt, kmean, batch):
    n = q.shape[0]
    s = n // batch
    nb = s // MOBA_BLOCK
    assert nb % MOBA_GROUP == 0
    return pl.pallas_call(
        functools.partial(_moba_prompt_kernel, nb=nb),
        grid=(batch, HEADS, nb),
        in_specs=[pl.BlockSpec((MOBA_BLOCK, DH), lambda b, h, j: (b * nb + j, h)),
                  pl.BlockSpec((s, DH), lambda b, h, j: (b, h)),
                  pl.BlockSpec((nb, DH, MOBA_BLOCK), lambda b, h, j: (b, h, 0)),
                  pl.BlockSpec((1, nb, DH), lambda b, h, j: (b, 0, h))],
        out_specs=pl.BlockSpec((MOBA_BLOCK, DH), lambda b, h, j: (b * nb + j, h)),
        out_shape=jax.ShapeDtypeStruct((n, HW), BF16),
        scratch_shapes=[pltpu.VMEM((nb, MOBA_BLOCK), F32), pltpu.VMEM((1, MOBA_BLOCK), F32),
                        pltpu.VMEM((1, MOBA_BLOCK), F32), pltpu.VMEM((DH, MOBA_BLOCK), F32),
                        pltpu.VMEM((MOBA_GROUP * MOBA_BLOCK, MOBA_BLOCK), F32),
                        pltpu.VMEM((MOBA_GROUP * MOBA_BLOCK, MOBA_BLOCK), F32),
                        pltpu.VMEM((MOBA_GROUP, MOBA_BLOCK, MOBA_BLOCK), BF16),
                        pltpu.VMEM((MOBA_GROUP, MOBA_BLOCK, MOBA_BLOCK), BF16)],
        compiler_params=_params("parallel", "parallel", "arbitrary"),
        name="moba_prompt",
    )(q, kb, vt, kmean.reshape(batch, nb, HW))


def _moba_sample_scores_kernel(pt_ref, q_ref, kn_ref, *rest, t, n_pages, page, pps):
    del pt_ref
    k_refs = rest[:pps]
    pp_ref, po_ref, s_ref, ksum_ref = rest[pps:]
    c = pl.program_id(1)
    nc = pl.num_programs(1)
    ppb = MOBA_BLOCK // page
    n_blk = n_pages // ppb
    fold = SUBLANES // HEADS
    scale = DH ** -0.5
    q = q_ref[...]
    qall = jnp.concatenate([q[:, h * DH:(h + 1) * DH] for h in range(HEADS)], axis=0)
    qbf = (qall * scale).astype(BF16)

    for jb in range(pps // ppb):
        ksum = None
        for jp in range(ppb):
            pg = jb * ppb + jp
            kp = k_refs[pg][0]
            s_ref[c * pps + pg] = _mm(qbf, kp, NT_DIMS)
            part = jnp.sum(kp.reshape(page * HEADS // SUBLANES, SUBLANES, DH), axis=0)
            ksum = part if ksum is None else ksum + part
        total = ksum
        for sh in range(1, fold):
            total = total + pltpu.roll(ksum, sh * HEADS, axis=0)
        ksum_ref[pl.ds(pl.multiple_of((c * (pps // ppb) + jb) * SUBLANES, SUBLANES), SUBLANES), :] = total

    @pl.when(c == nc - 1)
    def _():
        ht = HEADS * t
        kmean = ksum_ref[...] * (1.0 / MOBA_BLOCK)
        gate = _mm3(qall, kmean, NT_DIMS)
        grow = lax.broadcasted_iota(I32, gate.shape, 0)
        gcol = lax.broadcasted_iota(I32, gate.shape, 1)
        gate = jnp.where(gcol % SUBLANES == grow // t, gate, NEG_INF)
        picks = [(idx // SUBLANES, val > NEG_INF)
                 for idx, val in _top_picks(gate, MOBA_TOPK, gcol, n_blk * SUBLANES, 1)]

        def block_on(j):
            on = picks[0][1] & (picks[0][0] == j)
            for blk_id, ok in picks[1:]:
                on = on | (ok & (blk_id == j))
            return on

        s_own = _mm(qbf, kn_ref[...], NT_DIMS)
        ro = lax.broadcasted_iota(I32, s_own.shape, 0)
        co = lax.broadcasted_iota(I32, s_own.shape, 1)
        s_own = jnp.where((co % HEADS == ro // t) & (co // HEADS <= ro % t), s_own, NEG_INF)
        rp = lax.broadcasted_iota(I32, (ht, page * HEADS), 0)
        cp = lax.broadcasted_iota(I32, (ht, page * HEADS), 1)
        same_head = cp % HEADS == rp // t
        m = jnp.max(s_own, axis=1, keepdims=True)
        for pg in range(n_pages):
            keep = same_head & block_on(pg // ppb)
            m = jnp.maximum(m, jnp.max(jnp.where(keep, s_ref[pg], NEG_INF), axis=1, keepdims=True))
        p_own = jnp.exp(s_own - m)
        l = jnp.sum(p_own, axis=1, keepdims=True)
        for pg in range(n_pages):
            keep = same_head & block_on(pg // ppb)
            p = jnp.where(keep, jnp.exp(s_ref[pg] - m), 0.0)
            s_ref[pg] = p
            l = l + jnp.sum(p, axis=1, keepdims=True)
        inv = 1.0 / l
        po_ref[0] = p_own * inv
        for pg in range(n_pages):
            pp_ref[0, pg] = (s_ref[pg] * inv).astype(pp_ref.dtype)


def _moba_sample_apply_kernel(pt_ref, pp_ref, po_ref, vn_ref, *rest, t, pps):
    del pt_ref
    v_refs = rest[:pps]
    o_ref, acc_ref = rest[pps:]
    c = pl.program_id(1)
    nc = pl.num_programs(1)

    @pl.when(c == 0)
    def _():
        acc_ref[...] = _mm(po_ref[0], vn_ref[...])

    acc = acc_ref[...]
    for pg in range(pps):
        acc = acc + _mm(pp_ref[0, pg], v_refs[pg][0])
    acc_ref[...] = acc

    @pl.when(c == nc - 1)
    def _():
        for h in range(HEADS):
            o_ref[:, h * DH:(h + 1) * DH] = acc_ref[h * t:(h + 1) * t, :]


def _moba_sample(q, k_new, v_new, cache_k, cache_v, page_table):
    bd, n_pages = page_table.shape
    n = q.shape[0]
    t = n // bd
    prow = cache_k.shape[1]
    page = prow // HEADS
    pps = min(PAGES_PER_STEP, n_pages)
    assert n_pages % pps == 0 and (n_pages * page) % MOBA_BLOCK == 0 and MOBA_BLOCK % page == 0
    assert SUBLANES % HEADS == 0 and pps % (MOBA_BLOCK // page) == 0
    nc = n_pages // pps
    ht = HEADS * t
    n_blk = n_pages * page // MOBA_BLOCK
    pt = page_table.reshape(-1).astype(I32)
    seq_q = pl.BlockSpec((t, HW), lambda b, c, pt: (b, 0))
    seq_kv = pl.BlockSpec((ht, DH), lambda b, c, pt: (b, 0))
    own = pl.BlockSpec((1, ht, ht), lambda b, c, pt: (b, 0, 0))

    def page_spec(i):
        return pl.BlockSpec((1, prow, DH), lambda b, c, pt: (pt[b * n_pages + c * pps + i], 0, 0))

    p_past, p_own = pl.pallas_call(
        functools.partial(_moba_sample_scores_kernel, t=t, n_pages=n_pages, page=page, pps=pps),
        grid_spec=pltpu.PrefetchScalarGridSpec(
            num_scalar_prefetch=1,
            grid=(bd, nc),
            in_specs=[seq_q, seq_kv] + [page_spec(i) for i in range(pps)],
            out_specs=[pl.BlockSpec((1, n_pages, ht, prow), lambda b, c, pt: (b, 0, 0, 0)), own],
            scratch_shapes=[pltpu.VMEM((n_pages, ht, prow), F32),
                            pltpu.VMEM((n_blk * SUBLANES, DH), F32)]),
        out_shape=[jax.ShapeDtypeStruct((bd, n_pages, ht, prow), BF16),
                   jax.ShapeDtypeStruct((bd, ht, ht), F32)],
        compiler_params=_params("parallel", "arbitrary"),
        name="moba_sample_scores",
    )(pt, q, k_new, *([cache_k] * pps))

    return pl.pallas_call(
        functools.partial(_moba_sample_apply_kernel, t=t, pps=pps),
        grid_spec=pltpu.PrefetchScalarGridSpec(
            num_scalar_prefetch=1,
            grid=(bd, nc),
            in_specs=[pl.BlockSpec((1, pps, ht, prow), lambda b, c, pt: (b, c, 0, 0)), own, seq_kv]
                     + [page_spec(i) for i in range(pps)],
            out_specs=seq_q,
            scratch_shapes=[pltpu.VMEM((ht, DH), F32)]),
        out_shape=jax.ShapeDtypeStruct((n, HW), F32),
        compiler_params=_params("parallel", "arbitrary"),
        name="moba_sample_apply",
    )(pt, p_past, p_own, v_new, *([cache_v] * pps))


def _postmix_kernel(dn_ref, mb_ref, x_ref, wo1_ref, wo2_ref, nx_ref, wq_ref, mk_ref, mv_ref, wxo_ref,
                    nf_ref, wr_ref, br_ref, tri_ref,
                    x2_ref, h2_ref, tope_ref, gate_ref, rank_ref, cnt_ref, carry_ref, *, seqs, n_exp, n_mem):
    i = pl.program_id(0)

    @pl.when(i == 0)
    def _():
        carry_ref[...] = jnp.zeros_like(carry_ref)

    x1 = x_ref[...] + _mm(dn_ref[...], wo1_ref[...]) + _mm(mb_ref[...], wo2_ref[...])
    tm = x1.shape[0]
    q = _mm(_rms(x1, nx_ref[...]), wq_ref[...])
    scale = DH ** -0.5
    heads = []
    for h in range(HEADS):
        hs = slice(h * DH, (h + 1) * DH)
        mem_rows = pl.ds(h, n_mem, stride=HEADS)
        if seqs == 1:
            s = _mm(q[:, hs], mk_ref[0, mem_rows, :], NT_DIMS) * scale
            p = jnp.exp(s - jnp.max(s, axis=-1, keepdims=True))
            p = p / jnp.sum(p, axis=-1, keepdims=True)
            heads.append(_mm(p, mv_ref[0, mem_rows, :]))
        else:
            q3 = q[:, hs].reshape(seqs, tm // seqs, DH).astype(BF16)
            mk = jnp.stack([mk_ref[g, mem_rows, :] for g in range(seqs)]).astype(BF16)
            mv = jnp.stack([mv_ref[g, mem_rows, :] for g in range(seqs)]).astype(BF16)
            s = jnp.einsum("gtd,gmd->gtm", q3, mk, preferred_element_type=F32) * scale
            p = jnp.exp(s - jnp.max(s, axis=-1, keepdims=True))
            p = p / jnp.sum(p, axis=-1, keepdims=True)
            o = jnp.einsum("gtm,gmd->gtd", p.astype(BF16), mv, preferred_element_type=F32)
            heads.append(o.reshape(tm, DH))
    x2 = x1 + _mm(jnp.concatenate(heads, axis=1), wxo_ref[...])
    x2_ref[...] = x2
    h2 = _rms(x2, nf_ref[...])
    _store_row_tiles(h2_ref, h2)

    logits = _mm3(h2, wr_ref[...]) + br_ref[...]
    lt = logits.T[:n_exp]
    erow = lax.broadcasted_iota(I32, lt.shape, 0)
    picks = _top_picks(lt, TOP_K, erow, n_exp, 0)
    ex = [jnp.exp(val - picks[0][1]) for _, val in picks]
    den = ex[0]
    for e in ex[1:]:
        den = den + e
    onehot = [erow == idx for idx, _ in picks]
    any_hot = onehot[0]
    for oh in onehot[1:]:
        any_hot = any_hot | oh
    hot = any_hot.astype(F32)
    base = _mm(hot, tri_ref[...]) + carry_ref[:, 0:1]
    for k in range(TOP_K):
        tope_ref[k:k + 1, :] = picks[k][0]
        gate_ref[k:k + 1, :] = ex[k] / den
        rank_ref[k:k + 1, :] = jnp.sum(jnp.where(onehot[k], base, 0.0), axis=0, keepdims=True).astype(I32)
    pad = SUBLANES - TOP_K
    tope_ref[TOP_K:, :] = jnp.zeros((pad, tm), I32)
    gate_ref[TOP_K:, :] = jnp.zeros((pad, tm), F32)
    rank_ref[TOP_K:, :] = jnp.zeros((pad, tm), I32)
    carry = carry_ref[...] + jnp.sum(hot, axis=1, keepdims=True)
    carry_ref[...] = carry
    cnt_ref[...] = carry.astype(I32)


def _post_mix(dn, mb, x, mem_k, mem_v, w_out, norm_x, w_xq, w_xo, norm_ffn, w_router, b_router):
    n, d = x.shape
    n_seq = mem_k.shape[0]
    n_mem = mem_k.shape[1] // HEADS
    rows_per_seq = n // n_seq
    n_exp = w_router.shape[1]
    if rows_per_seq >= ROW_TILE:
        tm, seqs = ROW_TILE, 1
        assert rows_per_seq % tm == 0
        mem_map = lambda i: (i * tm // rows_per_seq, 0, 0)
    else:
        seqs = min(max(SAMPLE_SEQS, LANES // rows_per_seq), n_seq)
        tm = seqs * rows_per_seq
        mem_map = lambda i: (i, 0, 0)
    wo = w_out.astype(BF16)
    wr = jnp.pad(w_router, ((0, 0), (0, LANES - n_exp)))
    br = jnp.pad(b_router, (0, LANES - n_exp)).reshape(1, LANES)
    tri = (jnp.arange(tm)[:, None] < jnp.arange(tm)[None, :]).astype(BF16)
    row = lambda w: pl.BlockSpec((tm, w), lambda i: (i, 0))
    col = pl.BlockSpec((SUBLANES, tm), lambda i: (0, i))
    mem_spec = pl.BlockSpec((seqs, n_mem * HEADS, DH), mem_map)
    return pl.pallas_call(
        functools.partial(_postmix_kernel, seqs=seqs, n_exp=n_exp, n_mem=n_mem),
        grid=(n // tm,),
        in_specs=[row(HW), row(HW), row(d), _full((HW, d)), _full((HW, d)), _full((1, d)), _full((d, HW)),
                  mem_spec, mem_spec, _full((HW, d)), _full((1, d)), _full((d, LANES)), _full((1, LANES)),
                  _full((tm, tm))],
        out_specs=[row(d), pl.BlockSpec((tm * (d // LANES), LANES), lambda i: (i, 0)), col, col, col,
                   _full((n_exp, LANES))],
        out_shape=[jax.ShapeDtypeStruct((n, d), F32), jax.ShapeDtypeStruct((n * (d // LANES), LANES), F32),
                   jax.ShapeDtypeStruct((SUBLANES, n), I32), jax.ShapeDtypeStruct((SUBLANES, n), F32),
                   jax.ShapeDtypeStruct((SUBLANES, n), I32), jax.ShapeDtypeStruct((n_exp, LANES), I32)],
        scratch_shapes=[pltpu.VMEM((n_exp, LANES), F32)],
        compiler_params=_params("arbitrary"),
        name="post_mix",
    )(dn, mb, x, wo[:HW], wo[HW:], norm_x.reshape(1, d), w_xq.astype(BF16), mem_k, mem_v,
      w_xo.astype(BF16), norm_ffn.reshape(1, d), wr, br, tri)


def _store_row_tiles(ref, x):
    rows, width = x.shape
    rt = width // LANES
    for c in range(rt):
        ref[pl.ds(c, rows, stride=rt), :] = x[:, c * LANES:(c + 1) * LANES]


def _load_row_tiles(ref, rows):
    rt = ref.shape[0] // rows
    return jnp.concatenate([ref[pl.ds(c, rows, stride=rt), :] for c in range(rt)], axis=1)


def _plan_kernel(cnt_ref, tope_ref, rank_ref, dest_ref, be_ref, nv_ref, *, n_exp, bm, n_blocks):
    tope = tope_ref[...]
    dest = rank_ref[...]
    start = jnp.int32(0)
    first = pl.program_id(0) == 0
    for e in range(n_exp):
        dest = dest + jnp.where(tope == e, start * bm, 0)
        cnt_e = cnt_ref[e]
        nb_e = (cnt_e + (bm - 1)) // bm

        @pl.when(first)
        def _(e=e, start=start, nb_e=nb_e, cnt_e=cnt_e):
            def fill(j, c):
                be_ref[start + j] = e
                nv_ref[start + j] = jnp.minimum(cnt_e - j * bm, bm)
                return c
            lax.fori_loop(0, nb_e, fill, 0)

        start = start + nb_e
    dest_ref[...] = dest

    @pl.when(first)
    def _():
        def fill(j, c):
            be_ref[j] = n_exp - 1
            nv_ref[j] = 0
            return c
        lax.fori_loop(start, n_blocks, fill, 0)


def _moe_plan(cnt, tope, rank, bm, n_blocks):
    n = tope.shape[1]
    n_exp = cnt.shape[0]
    tl = min(2048, n)
    col = pl.BlockSpec((SUBLANES, tl), lambda i: (0, i))
    smem = lambda: pl.BlockSpec(memory_space=pltpu.SMEM)
    return pl.pallas_call(
        functools.partial(_plan_kernel, n_exp=n_exp, bm=bm, n_blocks=n_blocks),
        grid=(n // tl,),
        in_specs=[smem(), col, col],
        out_specs=[col, smem(), smem()],
        out_shape=[jax.ShapeDtypeStruct((SUBLANES, n), I32), jax.ShapeDtypeStruct((n_blocks,), I32),
                   jax.ShapeDtypeStruct((n_blocks,), I32)],
        compiler_params=_params("arbitrary"),
        name="moe_plan",
    )(cnt[:, 0], tope, rank)


def _tile_copy(src, src_row, dst, dst_row, sem):
    return pltpu.make_async_copy(src.at[pl.ds(src_row * SUBLANES, SUBLANES), :],
                                 dst.at[pl.ds(dst_row * SUBLANES, SUBLANES), :], sem)


def _dispatch_kernel(cnt_ref, dest_ref, x_ref, xs_ref, zero_ref, sem, zsem, *, tm, bm, n_exp, n_blocks):
    @pl.when(pl.program_id(0) == 0)
    def _():
        zero_ref[...] = jnp.zeros_like(zero_ref)

        def block_fill(j):
            return pltpu.make_async_copy(zero_ref, xs_ref.at[pl.ds(j * (bm * SUBLANES), bm * SUBLANES), :], zsem)

        def fills(act):
            start = jnp.int32(0)
            for e in range(n_exp):
                nb_e = (cnt_ref[e] + (bm - 1)) // bm
                start = start + nb_e

                @pl.when(nb_e > 0)
                def _(last=start - 1):
                    act(block_fill(last))

            def tail(j, c):
                act(block_fill(j))
                return c
            lax.fori_loop(start, n_blocks, tail, 0)

        fills(lambda cp: cp.start())
        fills(lambda cp: cp.wait())

    def issue(t, c):
        for k in range(TOP_K):
            _tile_copy(x_ref, t, xs_ref, dest_ref[k, t], sem).start(priority=k % 2)
        return c

    lax.fori_loop(0, tm, issue, 0, unroll=2)

    def drain(t, c):
        for k in range(TOP_K):
            _tile_copy(x_ref, t, xs_ref, dest_ref[k, t], sem).wait()
        return c

    lax.fori_loop(0, tm, drain, 0)


def _moe_dispatch(h2t, dest, cnt, bm, n_blocks):
    n = h2t.shape[0] // SUBLANES
    tm = min(MOVE_TILE, n)
    n_exp = cnt.shape[0]
    return pl.pallas_call(
        functools.partial(_dispatch_kernel, tm=tm, bm=bm, n_exp=n_exp, n_blocks=n_blocks),
        grid=(n // tm,),
        in_specs=[pl.BlockSpec(memory_space=pltpu.SMEM),
                  pl.BlockSpec((SUBLANES, tm), lambda i: (0, i), memory_space=pltpu.SMEM),
                  pl.BlockSpec((tm * SUBLANES, LANES), lambda i: (i, 0))],
        out_specs=pl.BlockSpec(memory_space=pl.ANY),
        out_shape=jax.ShapeDtypeStruct((n_blocks * bm * SUBLANES, LANES), F32),
        scratch_shapes=[pltpu.VMEM((bm * SUBLANES, LANES), F32), pltpu.SemaphoreType.DMA, pltpu.SemaphoreType.DMA],
        compiler_params=_params("arbitrary"),
        name="moe_dispatch",
    )(cnt[:, 0], dest, h2t)


def _expert_kernel(be_ref, nv_ref, x_ref, wg_ref, bg_ref, wu_ref, bu_ref, wd_ref, bd_ref, y_ref, *, bm):
    del be_ref
    n_valid = nv_ref[pl.program_id(0)]

    @pl.when(n_valid > 0)
    def _():
        x = _load_row_tiles(x_ref, bm).astype(BF16)
        g = jnp.minimum(_mm(x, wg_ref[0]) + bg_ref[0], SWIGLU_LIMIT)
        u = jnp.clip(_mm(x, wu_ref[0]) + bu_ref[0], -SWIGLU_LIMIT, SWIGLU_LIMIT)
        a = g * _sigmoid(SWIGLU_ALPHA * g) * (u + 1.0)
        _store_row_tiles(y_ref, _mm(a, wd_ref[0]) + bd_ref[0])

    @pl.when(n_valid <= 0)
    def _():
        y_ref[...] = jnp.zeros_like(y_ref)


def _moe_experts(xs, block_e, n_valid, w_gate, b_gate, w_up, b_up, w_down, b_down, bm):
    n_exp, d, d_ff = w_gate.shape
    assert d == SUBLANES * LANES, "a model row must be exactly one (SUBLANES, LANES) tile"
    n_blocks = xs.shape[0] // (bm * SUBLANES)
    wspec = lambda r, c: pl.BlockSpec((1, r, c), lambda i, be, nv: (be[i], 0, 0))
    rows = pl.BlockSpec((bm * SUBLANES, LANES), lambda i, be, nv: (i, 0))
    return pl.pallas_call(
        functools.partial(_expert_kernel, bm=bm),
        grid_spec=pltpu.PrefetchScalarGridSpec(
            num_scalar_prefetch=2,
            grid=(n_blocks,),
            in_specs=[rows, wspec(d, d_ff), wspec(1, d_ff), wspec(d, d_ff), wspec(1, d_ff), wspec(d_ff, d), wspec(1, d)],
            out_specs=rows),
        out_shape=jax.ShapeDtypeStruct(xs.shape, F32),
        compiler_params=_params("arbitrary"),
        name="moe_experts",
    )(block_e, n_valid, xs, w_gate.astype(BF16), b_gate.reshape(n_exp, 1, d_ff), w_up.astype(BF16),
      b_up.reshape(n_exp, 1, d_ff), w_down.astype(BF16), b_down.reshape(n_exp, 1, d))


def _combine_kernel(dest_ref, dnext_ref, gate_ref, x_ref, nw_ref, ys_ref, o_ref, buf_a, buf_b, sem_a, sem_b, *, tm):
    i = pl.program_id(0)
    n = pl.num_programs(0)

    def gather(d_ref, buf, sem, act):
        def body(t, c):
            for k in range(TOP_K):
                act(_tile_copy(ys_ref, d_ref[k, t], buf.at[k], t, sem), k)
            return c
        return body

    def start(d_ref, buf, sem):
        lax.fori_loop(0, tm, gather(d_ref, buf, sem, lambda cp, k: cp.start(priority=k % 2)), 0, unroll=2)

    def step(cur_buf, cur_sem, nxt_buf, nxt_sem):
        @pl.when(i + 1 < n)
        def _():
            start(dnext_ref, nxt_buf, nxt_sem)

        lax.fori_loop(0, tm, gather(dest_ref, cur_buf, cur_sem, lambda cp, k: cp.wait()), 0)
        gt = gate_ref[...].T
        acc = x_ref[...]
        for k in range(TOP_K):
            acc = acc + gt[:, k:k + 1] * _load_row_tiles(cur_buf.at[k], tm)
        o_ref[...] = _rms(acc, nw_ref[...])

    @pl.when(i == 0)
    def _():
        start(dest_ref, buf_a, sem_a)

    @pl.when(i % 2 == 0)
    def _():
        step(buf_a, sem_a, buf_b, sem_b)

    @pl.when(i % 2 == 1)
    def _():
        step(buf_b, sem_b, buf_a, sem_a)


def _moe_combine(ys, dest, gates, x2, norm_final):
    n, d = x2.shape
    tm = min(MOVE_TILE, n)
    nt = n // tm
    col = lambda ms: pl.BlockSpec((SUBLANES, tm), lambda i: (0, i), memory_space=ms)
    nxt = pl.BlockSpec((SUBLANES, tm), lambda i: (0, jnp.minimum(i + 1, nt - 1)), memory_space=pltpu.SMEM)
    buf = pltpu.VMEM((TOP_K, tm * SUBLANES, LANES), F32)
    return pl.pallas_call(
        functools.partial(_combine_kernel, tm=tm),
        grid=(nt,),
        in_specs=[col(pltpu.SMEM), nxt, col(pltpu.VMEM), pl.BlockSpec((tm, d), lambda i: (i, 0)), _full((1, d)),
                  pl.BlockSpec(memory_space=pl.ANY)],
        out_specs=pl.BlockSpec((tm, d), lambda i: (i, 0)),
        out_shape=jax.ShapeDtypeStruct((n, d), F32),
        scratch_shapes=[buf, buf, pltpu.SemaphoreType.DMA, pltpu.SemaphoreType.DMA],
        compiler_params=_params("arbitrary"),
        name="moe_combine",
    )(dest, dest, gates, x2, norm_final.reshape(1, d), ys)


def _moe_and_final_norm(x2, h2t, tope, gates, rank, cnt, w_gate, b_gate, w_up, b_up, w_down, b_down, norm_final):
    n = x2.shape[0]
    n_exp = w_gate.shape[0]
    bm = MOE_BLOCK_ROWS
    n_blocks = -(-(n * TOP_K + n_exp * (bm - 1)) // bm)
    dest, block_e, n_valid = _moe_plan(cnt, tope, rank, bm, n_blocks)
    xs = _moe_dispatch(h2t, dest, cnt, bm, n_blocks)
    ys = _moe_experts(xs, block_e, n_valid, w_gate, b_gate, w_up, b_up, w_down, b_down, bm)
    return _moe_combine(ys, dest, gates, x2, norm_final)


def kernel(x_prompt, x_sample, mem_prompt, cache_k, cache_v, page_table, state_delta, state_conv, cache_mem_k, cache_mem_v, norm_mix, w_in, conv_w, a_log, dt_bias, dn_norm, w_out, norm_x, norm_mem, w_xq, w_xk, w_xv, w_xo, norm_ffn, w_router, b_router, w_gate, b_gate, w_up, b_up, w_down, b_down, norm_final):
    depth = w_in.shape[0]
    assert depth == 1, "single-layer stack"
    b, s, d = x_prompt.shape
    bd, t, _ = x_sample.shape
    n_mem = mem_prompt.shape[1]
    n_pool, page = cache_k.shape[1], cache_k.shape[2]
    past = page_table.shape[1] * page
    assert past % MOBA_BLOCK == 0, "past length must be whole MoBA blocks"
    l = 0
    moe_w = (w_gate[l], b_gate[l], w_up[l], b_up[l], w_down[l], b_down[l])

    xs = x_sample.reshape(bd * t, d)
    tile_s = min(ROW_TILE, bd * t)
    cos_s, sin_s = _rope_tables(past + jnp.arange(tile_s) % t)
    conv_s, z_s, gb_s, q_s, k4_s, v4_s = _in_proj(xs, norm_mix[l], w_in[l], a_log[l], dt_bias[l],
                                                  cos_s, sin_s, prompt=False)
    dn_s, s_s, cv_s = _deltanet_sample(conv_s, z_s, gb_s, state_conv[l], state_delta[l], conv_w[l], dn_norm[l])
    mb_s = _moba_sample(q_s, k4_s, v4_s, cache_k.reshape(n_pool, page * HEADS, DH),
                        cache_v.reshape(n_pool, page * HEADS, DH), page_table)
    x2s, h2s, tope_s, gates_s, rank_s, cnt_s = _post_mix(
        dn_s, mb_s, xs, cache_mem_k.reshape(bd, n_mem * HEADS, DH), cache_mem_v.reshape(bd, n_mem * HEADS, DH),
        w_out[l], norm_x[l], w_xq[l], w_xo[l], norm_ffn[l], w_router[l], b_router[l])
    y_sample = _moe_and_final_norm(x2s, h2s, tope_s, gates_s, rank_s, cnt_s, *moe_w, norm_final)

    xp = x_prompt.reshape(b * s, d)
    cos_p, sin_p = _rope_tables(jnp.arange(s))
    conv_in, z, gb, q, k, v, kb, vt, kmean = _in_proj(xp, norm_mix[l], w_in[l], a_log[l], dt_bias[l],
                                                      cos_p, sin_p, prompt=True)
    dn, s_p, cv_p = _deltanet_prompt(conv_in, z, gb, conv_w[l], dn_norm[l], b)
    mb = _moba_prompt(q, kb, vt, kmean, b)
    mk_p, mv_p = _mem_kv(mem_prompt.reshape(b * n_mem, d), norm_mem[l], w_xk[l], w_xv[l])
    x2, h2, tope, gates, rank, cnt = _post_mix(
        dn, mb, xp, mk_p.reshape(b, n_mem * HEADS, DH), mv_p.reshape(b, n_mem * HEADS, DH), w_out[l], norm_x[l],
        w_xq[l], w_xo[l], norm_ffn[l], w_router[l], b_router[l])
    y_prompt = _moe_and_final_norm(x2, h2, tope, gates, rank, cnt, *moe_w, norm_final)

    hd = (HEADS, DH)
    return (y_prompt.reshape(b, s, d), y_sample.reshape(bd, t, d),
            k.reshape(1, b, s, *hd), v.reshape(1, b, s, *hd),
            k4_s.reshape(1, bd, t, *hd), v4_s.reshape(1, bd, t, *hd),
            s_p[None], s_s[None], cv_p[None], cv_s[None],
            mk_p.reshape(1, b, n_mem, *hd), mv_p.reshape(1, b, n_mem, *hd))
```

```python
import functools
import math

import jax
import jax.numpy as jnp
from jax import lax
from jax.experimental import pallas as pl
from jax.experimental.pallas import tpu as pltpu

F32 = jnp.float32
BF16 = jnp.bfloat16
I32 = jnp.int32

HEADS = 4
DH = 128
HW = HEADS * DH
CONV_W = 4
DN_CHUNK = 64
MOBA_BLOCK = 256
MOBA_TOPK = 3
ROPE_THETA = 10000.0
TOP_K = 4
SWIGLU_ALPHA = 1.702
SWIGLU_LIMIT = 7.0
EPS = 1e-6
NEG_INF = float("-inf")
LOG2_E = math.log2(math.e)

LANES = 128
SUBLANES = 8
VMEM_LIMIT = 56 * 1024 * 1024

ROW_TILE = 512
DN_GROUP = 256
MOBA_GROUP = 4
MOE_BLOCK_ROWS = 512
MOVE_TILE = 512
PAGES_PER_STEP = 16
SAMPLE_SEQS = 8

NN_DIMS = (((1,), (0,)), ((), ()))
NT_DIMS = (((1,), (1,)), ((), ()))
TN_DIMS = (((0,), (0,)), ((), ()))


def _mm(a, b, dims=NN_DIMS):
    return lax.dot_general(a.astype(BF16), b.astype(BF16), dims, preferred_element_type=F32)


def _split(a):
    hi = a.astype(BF16)
    lo = (a - hi.astype(F32)).astype(BF16)
    return hi, lo


def _mm3(a, b, dims=NN_DIMS):
    a_hi, a_lo = _split(a)
    b_hi, b_lo = _split(b)
    d = functools.partial(lax.dot_general, dimension_numbers=dims, preferred_element_type=F32)
    return d(a_hi, b_hi) + (d(a_hi, b_lo) + d(a_lo, b_hi))


def _rms(x, w):
    return x * lax.rsqrt(jnp.mean(x * x, axis=-1, keepdims=True) + EPS) * w


def _sigmoid(x):
    return 1.0 / (1.0 + jnp.exp(-x))


def _softplus(x):
    return jnp.maximum(x, 0.0) + jnp.log1p(jnp.exp(-jnp.abs(x)))


def _params(*sem):
    return pltpu.CompilerParams(dimension_semantics=sem, vmem_limit_bytes=VMEM_LIMIT)


def _full(shape):
    nd = len(shape)
    return pl.BlockSpec(shape, lambda *_: (0,) * nd)


def _top_picks(scores, n_pick, ids, n_ids, axis):
    picks = []
    cur = scores
    for _ in range(n_pick):
        m = jnp.max(cur, axis=axis, keepdims=True)
        idx = jnp.min(jnp.where(cur == m, ids, n_ids), axis=axis, keepdims=True)
        picks.append((idx, m))
        cur = jnp.where(ids == idx, NEG_INF, cur)
    return picks


def _memkv_kernel(m_ref, nw_ref, wk_ref, wv_ref, k_ref, v_ref):
    h = _rms(m_ref[...], nw_ref[...])
    rows = h.shape[0]
    k = _mm(h, wk_ref[...])
    v = _mm(h, wv_ref[...])
    for hd in range(HEADS):
        k_ref[pl.ds(hd, rows, stride=HEADS), :] = k[:, hd * DH:(hd + 1) * DH]
        v_ref[pl.ds(hd, rows, stride=HEADS), :] = v[:, hd * DH:(hd + 1) * DH]


def _mem_kv(mem, norm_mem, w_xk, w_xv):
    n, d = mem.shape
    tm = min(ROW_TILE, n)
    return pl.pallas_call(
        _memkv_kernel,
        grid=(n // tm,),
        in_specs=[pl.BlockSpec((tm, d), lambda i: (i, 0)), _full((1, d)), _full(w_xk.shape), _full(w_xv.shape)],
        out_specs=[pl.BlockSpec((tm * HEADS, DH), lambda i: (i, 0))] * 2,
        out_shape=[jax.ShapeDtypeStruct((n * HEADS, DH), F32)] * 2,
        compiler_params=_params("parallel"),
        name="mem_kv",
    )(mem, norm_mem.reshape(1, d), w_xk.astype(BF16), w_xv.astype(BF16))


def _rope(x, cos, sin):
    return x * cos + pltpu.roll(x, DH // 2, axis=1) * sin


def _inproj_kernel(x_ref, nw_ref, w_ref, wab_ref, cos_ref, sin_ref, alog_ref, dtb_ref,
                   conv_ref, z_ref, gb_ref, q_ref, k_ref, v_ref, *rest, conv_ch, prompt):
    x = x_ref[...]
    hb = _rms(x, nw_ref[...]).astype(BF16)
    conv_ref[...] = _mm(hb, w_ref[:, 0:conv_ch])
    z_ref[...] = _mm(hb, w_ref[:, conv_ch:conv_ch + HW])
    ab = _mm(hb, wab_ref[...])
    lane = lax.broadcasted_iota(I32, ab.shape, 1)
    g = -jnp.exp(alog_ref[...]) * _softplus(ab + dtb_ref[...])
    gb_ref[...] = jnp.where(lane < HEADS, g, _sigmoid(ab))
    off = conv_ch + HW
    cos = cos_ref[...]
    sin = sin_ref[...]
    q = _mm(hb, w_ref[:, off:off + HW])
    k = _mm(hb, w_ref[:, off + HW:off + 2 * HW])
    v = _mm(hb, w_ref[:, off + 2 * HW:off + 3 * HW])
    rows = x.shape[0]
    k_rot = []
    for h in range(HEADS):
        sl = slice(h * DH, (h + 1) * DH)
        q_ref[:, sl] = _rope(q[:, sl], cos, sin)
        k_rot.append(_rope(k[:, sl], cos, sin))
        k_ref[pl.ds(h, rows, stride=HEADS), :] = k_rot[h]
        v_ref[pl.ds(h, rows, stride=HEADS), :] = v[:, sl]
    if prompt:
        kb_ref, vt_ref, kmean_ref = rest
        for h in range(HEADS):
            kb_ref[:, h * DH:(h + 1) * DH] = k_rot[h].astype(BF16)
        for blk in range(x.shape[0] // MOBA_BLOCK):
            rows = slice(blk * MOBA_BLOCK, (blk + 1) * MOBA_BLOCK)
            vt_ref[blk] = v[rows].T.astype(BF16)
            for h in range(HEADS):
                kmean_ref[blk, :, h * DH:(h + 1) * DH] = jnp.mean(k_rot[h][rows], axis=0, keepdims=True)


def _in_proj(x, norm_w, w_in, a_log, dt_bias, cos, sin, *, prompt):
    n, d = x.shape
    conv_ch = 3 * HW
    tm = min(ROW_TILE, n)
    off_a = conv_ch + HW
    w_main = jnp.concatenate([w_in[:, :off_a], w_in[:, off_a + 2 * HEADS:]], axis=1).astype(BF16)
    w_ab = jnp.pad(w_in[:, off_a:off_a + 2 * HEADS], ((0, 0), (0, LANES - 2 * HEADS))).astype(BF16)
    alog = jnp.pad(a_log, (0, LANES - HEADS)).reshape(1, LANES)
    dtb = jnp.pad(dt_bias, (0, LANES - HEADS)).reshape(1, LANES)
    n_pos = cos.shape[0] // tm
    row = lambda w: pl.BlockSpec((tm, w), lambda i: (i, 0))
    kv_spec = pl.BlockSpec((tm * HEADS, DH), lambda i: (i, 0))
    out_specs = [row(conv_ch), row(HW), row(LANES), row(HW), kv_spec, kv_spec]
    out_shape = ([jax.ShapeDtypeStruct((n, w), F32) for w in (conv_ch, HW, LANES, HW)]
                 + [jax.ShapeDtypeStruct((n * HEADS, DH), F32)] * 2)
    if prompt:
        nb = tm // MOBA_BLOCK
        out_specs += [row(HW),
                      pl.BlockSpec((nb, HW, MOBA_BLOCK), lambda i: (i, 0, 0)),
                      pl.BlockSpec((nb, 1, HW), lambda i: (i, 0, 0))]
        out_shape += [jax.ShapeDtypeStruct((n, HW), BF16),
                      jax.ShapeDtypeStruct((n // MOBA_BLOCK, HW, MOBA_BLOCK), BF16),
                      jax.ShapeDtypeStruct((n // MOBA_BLOCK, 1, HW), F32)]
    return pl.pallas_call(
        functools.partial(_inproj_kernel, conv_ch=conv_ch, prompt=prompt),
        grid=(n // tm,),
        in_specs=[row(d), _full((1, d)), _full(w_main.shape), _full(w_ab.shape),
                  pl.BlockSpec((tm, DH), lambda i: (i % n_pos, 0)),
                  pl.BlockSpec((tm, DH), lambda i: (i % n_pos, 0)),
                  _full((1, LANES)), _full((1, LANES))],
        out_specs=out_specs,
        out_shape=out_shape,
        compiler_params=_params("parallel"),
        name="in_proj",
    )(x, norm_w.reshape(1, d), w_main, w_ab, cos, sin, alog, dtb)


def _rope_tables(pos):
    half = DH // 2
    inv = ROPE_THETA ** (-jnp.arange(half, dtype=F32) / half)
    ang = pos.astype(F32)[:, None] * inv[None, :]
    c, s = jnp.cos(ang), jnp.sin(ang)
    return jnp.concatenate([c, c], axis=1), jnp.concatenate([-s, s], axis=1)


def _chunk_cumsum(g, row_in_chunk):
    for sh in (1, 2, 4, 8, 16, 32):
        g = g + jnp.where(row_in_chunk >= sh, pltpu.roll(g, sh, axis=0), 0.0)
    return g


def _unit_lower_inverse(a):
    n = a.shape[0]
    eye = (lax.broadcasted_iota(I32, (n, n), 0) == lax.broadcasted_iota(I32, (n, n), 1)).astype(F32)
    p = eye - a
    ak = a
    for _ in range(5):
        ak = _mm(ak, ak)
        p = p + _mm(p, ak)
    return p


def _gated_out(o, z, dn_w):
    return _rms(o, dn_w) * (z * _sigmoid(z))


def _dn_prompt_kernel(u_ref, halo_ref, z_ref, gb_ref, cw_ref, dnw_ref,
                      o_ref, sfin_ref, cnew_ref,
                      ext_ref, s_ref, wq_ref, us_ref, kd_ref, vn_ref, oi_ref, qk_ref, gl_ref, *, tm):
    i = pl.program_id(1)
    nt = pl.num_programs(1)
    nc = tm // DN_CHUNK
    cpg = DN_GROUP // DN_CHUNK
    hal = CONV_W - 1

    @pl.when(i == 0)
    def _():
        s_ref[...] = jnp.zeros_like(s_ref)

    u = u_ref[...]
    ext_ref[0:SUBLANES, :] = jnp.where(i > 0, halo_ref[...], 0.0)
    ext_ref[SUBLANES:SUBLANES + tm, :] = u
    cw = cw_ref[...]
    y = u * cw[hal:hal + 1]
    for j in range(hal):
        y = y + ext_ref[SUBLANES - hal + j:SUBLANES - hal + j + tm, :] * cw[j:j + 1]
    qkv = y * _sigmoid(y)

    gb = gb_ref[...]
    row_in_chunk = lax.broadcasted_iota(I32, gb.shape, 0) % DN_CHUNK
    gc = _chunk_cumsum(gb, row_in_chunk)
    gc3 = gc.reshape(nc, DN_CHUNK, LANES)
    glast = jnp.broadcast_to(gc3[:, DN_CHUNK - 1:DN_CHUNK, :], gc3.shape).reshape(tm, LANES)
    gl_ref[...] = jnp.exp(glast)
    gct = gc.T

    gi = lax.broadcasted_iota(I32, (DN_GROUP, DN_GROUP), 0)
    gj = lax.broadcasted_iota(I32, (DN_GROUP, DN_GROUP), 1)
    same = (gi // DN_CHUNK) == (gj // DN_CHUNK)
    causal = same & (gi >= gj)
    strict = same & (gi > gj)

    for h in range(HEADS):
        qh = qkv[:, h * DH:(h + 1) * DH]
        kh = qkv[:, HW + h * DH:HW + (h + 1) * DH]
        vh = qkv[:, 2 * HW + h * DH:2 * HW + (h + 1) * DH]
        qn = qh * lax.rsqrt(jnp.sum(qh * qh, axis=-1, keepdims=True) + EPS) * (DH ** -0.5)
        kn = kh * lax.rsqrt(jnp.sum(kh * kh, axis=-1, keepdims=True) + EPS)
        gcol = gc[:, h:h + 1]
        beta = gb[:, HEADS + h:HEADS + h + 1]
        eg = jnp.exp(gcol)
        kbeta = kn * beta
        kd_ref[h] = kn * jnp.exp(glast[:, h:h + 1] - gcol)
        qe = qn * eg
        for r in range(tm // DN_GROUP):
            rs = slice(r * DN_GROUP, (r + 1) * DN_GROUP)
            decay = jnp.where(causal, jnp.exp(jnp.minimum(gcol[rs] - gct[h:h + 1, rs], 0.0)), 0.0)
            a = jnp.where(strict, _mm(kbeta[rs], kn[rs], NT_DIMS) * decay, 0.0)
            qk_ref[h, rs, :] = _mm(qn[rs], kn[rs], NT_DIMS) * decay
            t_inv = _unit_lower_inverse(a)
            rhs = jnp.concatenate([vh[rs] * beta[rs], kbeta[rs] * eg[rs]], axis=1)
            uw = _mm(t_inv, rhs)
            us_ref[h, rs, :] = uw[:, :DH]
            for c in range(cpg):
                lo = r * DN_GROUP + c * DN_CHUNK
                wq_ref[h, r * cpg + c, 0:DN_CHUNK, :] = uw[c * DN_CHUNK:(c + 1) * DN_CHUNK, DH:]
                wq_ref[h, r * cpg + c, DN_CHUNK:2 * DN_CHUNK, :] = qe[lo:lo + DN_CHUNK]

    def chunk_step(c, carry):
        r0 = pl.multiple_of(c * DN_CHUNK, DN_CHUNK)
        rows = pl.ds(r0, DN_CHUNK)
        for h in range(HEADS):
            s = s_ref[h]
            m1 = _mm(wq_ref[h, c], s)
            v_new = us_ref[h, rows, :] - m1[:DN_CHUNK]
            oi_ref[h, rows, :] = m1[DN_CHUNK:]
            vn_ref[h, rows, :] = v_new
            s_ref[h] = s * gl_ref[pl.ds(r0, 1), h:h + 1] + _mm(kd_ref[h, rows, :], v_new, TN_DIMS)
        return carry

    lax.fori_loop(0, nc, chunk_step, 0)

    z = z_ref[...]
    dnw = dnw_ref[...]
    for h in range(HEADS):
        parts = []
        for r in range(tm // DN_GROUP):
            rs = slice(r * DN_GROUP, (r + 1) * DN_GROUP)
            parts.append(oi_ref[h, rs, :] + _mm(qk_ref[h, rs, :], vn_ref[h, rs, :]))
        o = jnp.concatenate(parts, axis=0)
        o_ref[:, h * DH:(h + 1) * DH] = _gated_out(o, z[:, h * DH:(h + 1) * DH], dnw).astype(o_ref.dtype)

    @pl.when(i == nt - 1)
    def _():
        sfin_ref[0] = s_ref[...]
        cnew_ref[0] = ext_ref[SUBLANES + tm - hal:SUBLANES + tm, :]


def _deltanet_prompt(conv_in, z, gb, conv_w, dn_norm, batch):
    n, conv_ch = conv_in.shape
    s = n // batch
    tm = min(ROW_TILE, s)
    nt = s // tm
    hal = CONV_W - 1
    hb = tm // SUBLANES
    row = lambda w: pl.BlockSpec((tm, w), lambda b, i: (b * nt + i, 0))
    head_scr = lambda w: pltpu.VMEM((HEADS, tm, w), F32)
    return pl.pallas_call(
        functools.partial(_dn_prompt_kernel, tm=tm),
        grid=(batch, nt),
        in_specs=[row(conv_ch),
                  pl.BlockSpec((SUBLANES, conv_ch), lambda b, i: (jnp.maximum((b * nt + i) * hb - 1, 0), 0)),
                  row(HW), row(LANES), _full((CONV_W, conv_ch)), _full((1, DH))],
        out_specs=[row(HW),
                   pl.BlockSpec((1, HEADS, DH, DH), lambda b, i: (b, 0, 0, 0)),
                   pl.BlockSpec((1, hal, conv_ch), lambda b, i: (b, 0, 0))],
        out_shape=[jax.ShapeDtypeStruct((n, HW), BF16),
                   jax.ShapeDtypeStruct((batch, HEADS, DH, DH), F32),
                   jax.ShapeDtypeStruct((batch, hal, conv_ch), F32)],
        scratch_shapes=[pltpu.VMEM((SUBLANES + tm, conv_ch), F32),
                        pltpu.VMEM((HEADS, DH, DH), F32),
                        pltpu.VMEM((HEADS, tm // DN_CHUNK, 2 * DN_CHUNK, DH), F32),
                        head_scr(DH), head_scr(DH), head_scr(DH), head_scr(DH), head_scr(DN_GROUP),
                        pltpu.VMEM((tm, LANES), F32)],
        compiler_params=_params("arbitrary", "arbitrary"),
        name="deltanet_prompt",
    )(conv_in, conv_in, z, gb, conv_w, dn_norm.reshape(1, DH))


def _dn_sample_kernel(u_ref, cbuf_ref, z_ref, gb_ref, st_ref, cw_ref, dnw_ref,
                      o_ref, snew_ref, cnew_ref, ext_ref, *, t, seqs):
    hal = CONV_W - 1
    cw = cw_ref[...]
    dnw = dnw_ref[...]

    def seq_step(si, carry):
        r0 = pl.multiple_of(si * t, t)
        rows = pl.ds(r0, t)
        ext_ref[SUBLANES - hal:SUBLANES, :] = cbuf_ref[si]
        ext_ref[SUBLANES:SUBLANES + t, :] = u_ref[rows, :]
        y = ext_ref[SUBLANES - hal:SUBLANES - hal + t, :] * cw[0:1]
        for j in range(1, CONV_W):
            y = y + ext_ref[SUBLANES - hal + j:SUBLANES - hal + j + t, :] * cw[j:j + 1]
        cnew_ref[si] = ext_ref[SUBLANES + t - hal:SUBLANES + t, :]
        qkv = y * _sigmoid(y)
        gb = gb_ref[rows, :]
        z = z_ref[rows, :]
        for h in range(HEADS):
            qh = qkv[:, h * DH:(h + 1) * DH]
            kh = qkv[:, HW + h * DH:HW + (h + 1) * DH]
            vh = qkv[:, 2 * HW + h * DH:2 * HW + (h + 1) * DH]
            qn = qh * lax.rsqrt(jnp.sum(qh * qh, axis=-1, keepdims=True) + EPS) * (DH ** -0.5)
            kn = kh * lax.rsqrt(jnp.sum(kh * kh, axis=-1, keepdims=True) + EPS)
            qt = qn.T
            kt = kn.T
            eg = jnp.exp(gb[:, h:h + 1])
            beta = gb[:, HEADS + h:HEADS + h + 1]
            s = st_ref[si, h]
            outs = []
            for ti in range(t):
                kcol = kt[:, ti:ti + 1]
                s = s * eg[ti:ti + 1]
                v_new = beta[ti:ti + 1] * (vh[ti:ti + 1] - jnp.sum(s * kcol, axis=0, keepdims=True))
                s = s + kcol * v_new
                outs.append(jnp.sum(s * qt[:, ti:ti + 1], axis=0, keepdims=True))
            snew_ref[si, h] = s
            o = jnp.concatenate(outs, axis=0)
            o_ref[rows, h * DH:(h + 1) * DH] = _gated_out(o, z[:, h * DH:(h + 1) * DH], dnw)
        return carry

    lax.fori_loop(0, seqs, seq_step, 0)


def _deltanet_sample(conv_in, z, gb, state_conv, state_delta, conv_w, dn_norm):
    n, conv_ch = conv_in.shape
    bd = state_delta.shape[0]
    t = n // bd
    seqs = min(SAMPLE_SEQS, bd)
    hal = CONV_W - 1
    row = lambda w: pl.BlockSpec((seqs * t, w), lambda i: (i, 0))
    st_spec = pl.BlockSpec((seqs, HEADS, DH, DH), lambda i: (i, 0, 0, 0))
    cb_spec = pl.BlockSpec((seqs, hal, conv_ch), lambda i: (i, 0, 0))
    return pl.pallas_call(
        functools.partial(_dn_sample_kernel, t=t, seqs=seqs),
        grid=(bd // seqs,),
        in_specs=[row(conv_ch), cb_spec, row(HW), row(LANES), st_spec, _full((CONV_W, conv_ch)), _full((1, DH))],
        out_specs=[row(HW), st_spec, cb_spec],
        out_shape=[jax.ShapeDtypeStruct((n, HW), F32),
                   jax.ShapeDtypeStruct(state_delta.shape, F32),
                   jax.ShapeDtypeStruct(state_conv.shape, F32)],
        scratch_shapes=[pltpu.VMEM((SUBLANES + t, conv_ch), F32)],
        compiler_params=_params("parallel"),
        name="deltanet_sample",
    )(conv_in, state_conv, z, gb, state_delta, conv_w, dn_norm.reshape(1, DH))


def _moba_prompt_kernel(q_ref, k_ref, vt_ref, km_ref, o_ref, sel_ref, m_ref, l_ref, acc_ref,
                        sa_ref, sb_ref, pa_ref, pb_ref, *, nb):
    qb = pl.program_id(2)
    blk = MOBA_BLOCK
    grp = MOBA_GROUP
    q = q_ref[...]
    qs = (q * (DH ** -0.5 * LOG2_E)).astype(BF16)

    gs = _mm3(km_ref[0], q, NT_DIMS)
    brow = lax.broadcasted_iota(I32, gs.shape, 0)
    gs = jnp.where(brow < qb, gs, NEG_INF)
    sel = jnp.zeros(gs.shape, F32)
    for idx, val in _top_picks(gs, MOBA_TOPK, brow, nb, 0):
        sel = jnp.where((brow == idx) & (val > NEG_INF), 1.0, sel)
    sel_ref[...] = sel

    def scores(row0, rows):
        kj = k_ref[pl.ds(pl.multiple_of(row0, blk), rows), :]
        return lax.dot_general(kj, qs, NT_DIMS, preferred_element_type=F32)

    ki = lax.broadcasted_iota(I32, (blk, blk), 0)
    qi = lax.broadcasted_iota(I32, (blk, blk), 1)
    s = jnp.where(ki <= qi, scores(qb * blk, blk), NEG_INF)
    m = jnp.max(s, axis=0, keepdims=True)
    p = jnp.exp2(s - m)
    m_ref[...] = m
    l_ref[...] = jnp.sum(p, axis=0, keepdims=True)
    acc_ref[...] = _mm(vt_ref[qb], p)

    n_groups = (qb + grp - 1) // grp
    last_group = nb // grp - 1
    sa_ref[...] = scores(0, grp * blk)
    pb_ref[...] = jnp.zeros_like(pb_ref)

    def apply_probs(g, p_ref):
        acc = acc_ref[...]
        for b in range(grp):
            acc = acc + lax.dot_general(vt_ref[g * grp + b], p_ref[b], NN_DIMS, preferred_element_type=F32)
        return acc

    def step(g, s_cur, s_next, p_prev, p_cur):
        s_next[...] = scores(jnp.minimum(g + 1, last_group) * (grp * blk), grp * blk)
        acc = apply_probs(jnp.maximum(g - 1, 0), p_prev)
        j0 = g * grp
        m_old = m_ref[...]
        m_new = m_old
        picked = []
        for b in range(grp):
            on = sel_ref[pl.ds(j0 + b, 1), :] > 0.5
            col_max = jnp.max(s_cur[b * blk:(b + 1) * blk, :], axis=0, keepdims=True)
            m_new = jnp.maximum(m_new, jnp.where(on, col_max, NEG_INF))
            picked.append(on)
        alpha = jnp.exp2(m_old - m_new)
        l = alpha * l_ref[...]
        for b in range(grp):
            p = jnp.exp2(s_cur[b * blk:(b + 1) * blk, :] - jnp.where(picked[b], m_new, -NEG_INF))
            l = l + jnp.sum(p, axis=0, keepdims=True)
            p_cur[b] = p.astype(BF16)
        m_ref[...] = m_new
        l_ref[...] = l
        acc_ref[...] = alpha * acc

    def group_pair(i, carry):
        step(2 * i, sa_ref, sb_ref, pb_ref, pa_ref)
        step(2 * i + 1, sb_ref, sa_ref, pa_ref, pb_ref)
        return carry

    lax.fori_loop(0, n_groups // 2, group_pair, 0)
    odd = n_groups % 2 == 1

    @pl.when(odd)
    def _():
        step(n_groups - 1, sa_ref, sb_ref, pb_ref, pa_ref)
        o_ref[...] = (apply_probs(n_groups - 1, pa_ref) / l_ref[...]).T.astype(o_ref.dtype)

    @pl.when(jnp.logical_not(odd))
    def _():
        o_ref[...] = (apply_probs(jnp.maximum(n_groups - 1, 0), pb_ref) / l_ref[...]).T.astype(o_ref.dtype)


def _moba_prompt(q, kb, vt, kmean, batch):
    n = q.shape[0]
    s = n // batch
    nb = s // MOBA_BLOCK
    assert nb % MOBA_GROUP == 0
    return pl.pallas_call(
        functools.partial(_moba_prompt_kernel, nb=nb),
        grid=(batch, HEADS, nb),
        in_specs=[pl.BlockSpec((MOBA_BLOCK, DH), lambda b, h, j: (b * nb + j, h)),
                  pl.BlockSpec((s, DH), lambda b, h, j: (b, h)),
                  pl.BlockSpec((nb, DH, MOBA_BLOCK), lambda b, h, j: (b, h, 0)),
                  pl.BlockSpec((1, nb, DH), lambda b, h, j: (b, 0, h))],
        out_specs=pl.BlockSpec((MOBA_BLOCK, DH), lambda b, h, j: (b * nb + j, h)),
        out_shape=jax.ShapeDtypeStruct((n, HW), BF16),
        scratch_shapes=[pltpu.VMEM((nb, MOBA_BLOCK), F32), pltpu.VMEM((1, MOBA_BLOCK), F32),
                        pltpu.VMEM((1, MOBA_BLOCK), F32), pltpu.VMEM((DH, MOBA_BLOCK), F32),
                        pltpu.VMEM((MOBA_GROUP * MOBA_BLOCK, MOBA_BLOCK), F32),
                        pltpu.VMEM((MOBA_GROUP * MOBA_BLOCK, MOBA_BLOCK), F32),
                        pltpu.VMEM((MOBA_GROUP, MOBA_BLOCK, MOBA_BLOCK), BF16),
                        pltpu.VMEM((MOBA_GROUP, MOBA_BLOCK, MOBA_BLOCK), BF16)],
        compiler_params=_params("parallel", "parallel", "arbitrary"),
        name="moba_prompt",
    )(q, kb, vt, kmean.reshape(batch, nb, HW))


def _moba_sample_scores_kernel(pt_ref, q_ref, kn_ref, *rest, t, n_pages, page, pps):
    del pt_ref
    k_refs = rest[:pps]
    pp_ref, po_ref, li_ref, s_ref, ksum_ref = rest[pps:]
    c = pl.program_id(1)
    nc = pl.num_programs(1)
    ppb = MOBA_BLOCK // page
    n_blk = n_pages // ppb
    fold = SUBLANES // HEADS
    scale = DH ** -0.5
    q = q_ref[...]
    qall = jnp.concatenate([q[:, h * DH:(h + 1) * DH] for h in range(HEADS)], axis=0)
    qbf = (qall * scale).astype(BF16)

    for jb in range(pps // ppb):
        ksum = None
        for jp in range(ppb):
            pg = jb * ppb + jp
            kp = k_refs[pg][0]
            s_ref[c * pps + pg] = _mm(qbf, kp, NT_DIMS)
            part = jnp.sum(kp.reshape(page * HEADS // SUBLANES, SUBLANES, DH), axis=0)
            ksum = part if ksum is None else ksum + part
        total = ksum
        for sh in range(1, fold):
            total = total + pltpu.roll(ksum, sh * HEADS, axis=0)
        ksum_ref[pl.ds(pl.multiple_of((c * (pps // ppb) + jb) * SUBLANES, SUBLANES), SUBLANES), :] = total

    @pl.when(c == nc - 1)
    def _():
        ht = HEADS * t
        kmean = ksum_ref[...] * (1.0 / MOBA_BLOCK)
        gate = _mm3(qall, kmean, NT_DIMS)
        grow = lax.broadcasted_iota(I32, gate.shape, 0)
        gcol = lax.broadcasted_iota(I32, gate.shape, 1)
        gate = jnp.where(gcol % SUBLANES == grow // t, gate, NEG_INF)
        picks = [(idx // SUBLANES, val > NEG_INF)
                 for idx, val in _top_picks(gate, MOBA_TOPK, gcol, n_blk * SUBLANES, 1)]

        def block_on(j):
            on = picks[0][1] & (picks[0][0] == j)
            for blk_id, ok in picks[1:]:
                on = on | (ok & (blk_id == j))
            return on

        s_own = _mm(qbf, kn_ref[...], NT_DIMS)
        ro = lax.broadcasted_iota(I32, s_own.shape, 0)
        co = lax.broadcasted_iota(I32, s_own.shape, 1)
        s_own = jnp.where((co % HEADS == ro // t) & (co // HEADS <= ro % t), s_own, NEG_INF)
        rp = lax.broadcasted_iota(I32, (ht, page * HEADS), 0)
        cp = lax.broadcasted_iota(I32, (ht, page * HEADS), 1)
        same_head = cp % HEADS == rp // t
        m = jnp.max(s_own, axis=1, keepdims=True)
        for pg in range(n_pages):
            keep = same_head & block_on(pg // ppb)
            m = jnp.maximum(m, jnp.max(jnp.where(keep, s_ref[pg], NEG_INF), axis=1, keepdims=True))
        p_own = jnp.exp(s_own - m)
        l = jnp.sum(p_own, axis=1, keepdims=True)
        for pg in range(n_pages):
            keep = same_head & block_on(pg // ppb)
            p = jnp.where(keep, jnp.exp(s_ref[pg] - m), 0.0)
            pp_ref[0, pg] = p.astype(pp_ref.dtype)
            l = l + jnp.sum(p, axis=1, keepdims=True)
        po_ref[0] = p_own
        li_ref[0] = jnp.broadcast_to(1.0 / l, li_ref.shape[1:])


def _moba_sample_apply_kernel(pt_ref, pp_ref, po_ref, li_ref, vn_ref, *rest, t, pps):
    del pt_ref
    v_refs = rest[:pps]
    o_ref, acc_ref = rest[pps:]
    c = pl.program_id(1)
    nc = pl.num_programs(1)

    @pl.when(c == 0)
    def _():
        acc_ref[...] = _mm(po_ref[0], vn_ref[...])

    acc = acc_ref[...]
    for pg in range(pps):
        acc = acc + _mm(pp_ref[0, pg], v_refs[pg][0])
    acc_ref[...] = acc

    @pl.when(c == nc - 1)
    def _():
        for h in range(HEADS):
            o_ref[:, h * DH:(h + 1) * DH] = acc_ref[h * t:(h + 1) * t, :] * li_ref[0, h * t:(h + 1) * t, 0:1]


def _moba_sample(q, k_new, v_new, cache_k, cache_v, page_table):
    bd, n_pages = page_table.shape
    n = q.shape[0]
    t = n // bd
    prow = cache_k.shape[1]
    page = prow // HEADS
    pps = min(PAGES_PER_STEP, n_pages)
    assert n_pages % pps == 0 and (n_pages * page) % MOBA_BLOCK == 0 and MOBA_BLOCK % page == 0
    assert SUBLANES % HEADS == 0 and pps % (MOBA_BLOCK // page) == 0
    nc = n_pages // pps
    ht = HEADS * t
    n_blk = n_pages * page // MOBA_BLOCK
    pt = page_table.reshape(-1).astype(I32)
    seq_q = pl.BlockSpec((t, HW), lambda b, c, pt: (b, 0))
    seq_kv = pl.BlockSpec((ht, DH), lambda b, c, pt: (b, 0))
    own = pl.BlockSpec((1, ht, ht), lambda b, c, pt: (b, 0, 0))

    def page_spec(i):
        return pl.BlockSpec((1, prow, DH), lambda b, c, pt: (pt[b * n_pages + c * pps + i], 0, 0))

    inv_l = pl.BlockSpec((1, ht, LANES), lambda b, c, pt: (b, 0, 0))
    p_past, p_own, l_inv = pl.pallas_call(
        functools.partial(_moba_sample_scores_kernel, t=t, n_pages=n_pages, page=page, pps=pps),
        grid_spec=pltpu.PrefetchScalarGridSpec(
            num_scalar_prefetch=1,
            grid=(bd, nc),
            in_specs=[seq_q, seq_kv] + [page_spec(i) for i in range(pps)],
            out_specs=[pl.BlockSpec((1, n_pages, ht, prow), lambda b, c, pt: (b, 0, 0, 0)), own, inv_l],
            scratch_shapes=[pltpu.VMEM((n_pages, ht, prow), F32),
                            pltpu.VMEM((n_blk * SUBLANES, DH), F32)]),
        out_shape=[jax.ShapeDtypeStruct((bd, n_pages, ht, prow), BF16),
                   jax.ShapeDtypeStruct((bd, ht, ht), F32),
                   jax.ShapeDtypeStruct((bd, ht, LANES), F32)],
        compiler_params=_params("parallel", "arbitrary"),
        name="moba_sample_scores",
    )(pt, q, k_new, *([cache_k] * pps))

    return pl.pallas_call(
        functools.partial(_moba_sample_apply_kernel, t=t, pps=pps),
        grid_spec=pltpu.PrefetchScalarGridSpec(
            num_scalar_prefetch=1,
            grid=(bd, nc),
            in_specs=[pl.BlockSpec((1, pps, ht, prow), lambda b, c, pt: (b, c, 0, 0)), own, inv_l, seq_kv]
                     + [page_spec(i) for i in range(pps)],
            out_specs=seq_q,
            scratch_shapes=[pltpu.VMEM((ht, DH), F32)]),
        out_shape=jax.ShapeDtypeStruct((n, HW), F32),
        compiler_params=_params("parallel", "arbitrary"),
        name="moba_sample_apply",
    )(pt, p_past, p_own, l_inv, v_new, *([cache_v] * pps))


def _postmix_kernel(dn_ref, mb_ref, x_ref, wo1_ref, wo2_ref, nx_ref, wq_ref, mk_ref, mv_ref, wxo_ref,
                    nf_ref, wr_ref, br_ref, tri_ref,
                    x2_ref, h2_ref, tope_ref, gate_ref, rank_ref, cnt_ref, carry_ref, *, seqs, n_exp, n_mem):
    i = pl.program_id(0)

    @pl.when(i == 0)
    def _():
        carry_ref[...] = jnp.zeros_like(carry_ref)

    x1 = x_ref[...] + _mm(dn_ref[...], wo1_ref[...]) + _mm(mb_ref[...], wo2_ref[...])
    tm = x1.shape[0]
    q = _mm(_rms(x1, nx_ref[...]), wq_ref[...])
    scale = DH ** -0.5
    heads = []
    for h in range(HEADS):
        hs = slice(h * DH, (h + 1) * DH)
        mem_rows = pl.ds(h, n_mem, stride=HEADS)
        if seqs == 1:
            s = _mm(q[:, hs], mk_ref[0, mem_rows, :], NT_DIMS) * scale
            p = jnp.exp(s - jnp.max(s, axis=-1, keepdims=True))
            p = p / jnp.sum(p, axis=-1, keepdims=True)
            heads.append(_mm(p, mv_ref[0, mem_rows, :]))
        else:
            q3 = q[:, hs].reshape(seqs, tm // seqs, DH).astype(BF16)
            mk = jnp.stack([mk_ref[g, mem_rows, :] for g in range(seqs)]).astype(BF16)
            mv = jnp.stack([mv_ref[g, mem_rows, :] for g in range(seqs)]).astype(BF16)
            s = jnp.einsum("gtd,gmd->gtm", q3, mk, preferred_element_type=F32) * scale
            p = jnp.exp(s - jnp.max(s, axis=-1, keepdims=True))
            p = p / jnp.sum(p, axis=-1, keepdims=True)
            o = jnp.einsum("gtm,gmd->gtd", p.astype(BF16), mv, preferred_element_type=F32)
            heads.append(o.reshape(tm, DH))
    x2 = x1 + _mm(jnp.concatenate(heads, axis=1), wxo_ref[...])
    x2_ref[...] = x2
    h2 = _rms(x2, nf_ref[...])
    _store_row_tiles(h2_ref, h2)

    logits = _mm3(h2, wr_ref[...]) + br_ref[...]
    lt = logits.T[:n_exp]
    erow = lax.broadcasted_iota(I32, lt.shape, 0)
    picks = _top_picks(lt, TOP_K, erow, n_exp, 0)
    ex = [jnp.exp(val - picks[0][1]) for _, val in picks]
    den = ex[0]
    for e in ex[1:]:
        den = den + e
    onehot = [erow == idx for idx, _ in picks]
    any_hot = onehot[0]
    for oh in onehot[1:]:
        any_hot = any_hot | oh
    hot = any_hot.astype(F32)
    base = _mm(hot, tri_ref[...]) + carry_ref[:, 0:1]
    for k in range(TOP_K):
        tope_ref[k:k + 1, :] = picks[k][0]
        gate_ref[k:k + 1, :] = ex[k] / den
        rank_ref[k:k + 1, :] = jnp.sum(jnp.where(onehot[k], base, 0.0), axis=0, keepdims=True).astype(I32)
    pad = SUBLANES - TOP_K
    tope_ref[TOP_K:, :] = jnp.zeros((pad, tm), I32)
    gate_ref[TOP_K:, :] = jnp.zeros((pad, tm), F32)
    rank_ref[TOP_K:, :] = jnp.zeros((pad, tm), I32)
    carry = carry_ref[...] + jnp.sum(hot, axis=1, keepdims=True)
    carry_ref[...] = carry
    cnt_ref[...] = carry.astype(I32)


def _post_mix(dn, mb, x, mem_k, mem_v, w_out, norm_x, w_xq, w_xo, norm_ffn, w_router, b_router):
    n, d = x.shape
    n_seq = mem_k.shape[0]
    n_mem = mem_k.shape[1] // HEADS
    rows_per_seq = n // n_seq
    n_exp = w_router.shape[1]
    if rows_per_seq >= ROW_TILE:
        tm, seqs = ROW_TILE, 1
        assert rows_per_seq % tm == 0
        mem_map = lambda i: (i * tm // rows_per_seq, 0, 0)
    else:
        seqs = min(max(SAMPLE_SEQS, LANES // rows_per_seq), n_seq)
        tm = seqs * rows_per_seq
        mem_map = lambda i: (i, 0, 0)
    wo = w_out.astype(BF16)
    wr = jnp.pad(w_router, ((0, 0), (0, LANES - n_exp)))
    br = jnp.pad(b_router, (0, LANES - n_exp)).reshape(1, LANES)
    tri = (jnp.arange(tm)[:, None] < jnp.arange(tm)[None, :]).astype(BF16)
    row = lambda w: pl.BlockSpec((tm, w), lambda i: (i, 0))
    col = pl.BlockSpec((SUBLANES, tm), lambda i: (0, i))
    mem_spec = pl.BlockSpec((seqs, n_mem * HEADS, DH), mem_map)
    return pl.pallas_call(
        functools.partial(_postmix_kernel, seqs=seqs, n_exp=n_exp, n_mem=n_mem),
        grid=(n // tm,),
        in_specs=[row(HW), row(HW), row(d), _full((HW, d)), _full((HW, d)), _full((1, d)), _full((d, HW)),
                  mem_spec, mem_spec, _full((HW, d)), _full((1, d)), _full((d, LANES)), _full((1, LANES)),
                  _full((tm, tm))],
        out_specs=[row(d), pl.BlockSpec((tm * (d // LANES), LANES), lambda i: (i, 0)), col, col, col,
                   _full((n_exp, LANES))],
        out_shape=[jax.ShapeDtypeStruct((n, d), F32), jax.ShapeDtypeStruct((n * (d // LANES), LANES), F32),
                   jax.ShapeDtypeStruct((SUBLANES, n), I32), jax.ShapeDtypeStruct((SUBLANES, n), F32),
                   jax.ShapeDtypeStruct((SUBLANES, n), I32), jax.ShapeDtypeStruct((n_exp, LANES), I32)],
        scratch_shapes=[pltpu.VMEM((n_exp, LANES), F32)],
        compiler_params=_params("arbitrary"),
        name="post_mix",
    )(dn, mb, x, wo[:HW], wo[HW:], norm_x.reshape(1, d), w_xq.astype(BF16), mem_k, mem_v,
      w_xo.astype(BF16), norm_ffn.reshape(1, d), wr, br, tri)


def _store_row_tiles(ref, x):
    rows, width = x.shape
    rt = width // LANES
    for c in range(rt):
        ref[pl.ds(c, rows, stride=rt), :] = x[:, c * LANES:(c + 1) * LANES]


def _load_row_tiles(ref, rows):
    rt = ref.shape[0] // rows
    return jnp.concatenate([ref[pl.ds(c, rows, stride=rt), :] for c in range(rt)], axis=1)


def _plan_kernel(cnt_ref, tope_ref, rank_ref, dest_ref, be_ref, nv_ref, *, n_exp, bm, n_blocks):
    tope = tope_ref[...]
    dest = rank_ref[...]
    start = jnp.int32(0)
    first = pl.program_id(0) == 0
    for e in range(n_exp):
        dest = dest + jnp.where(tope == e, start * bm, 0)
        cnt_e = cnt_ref[e]
        nb_e = (cnt_e + (bm - 1)) // bm

        @pl.when(first)
        def _(e=e, start=start, nb_e=nb_e, cnt_e=cnt_e):
            def fill(j, c):
                be_ref[start + j] = e
                nv_ref[start + j] = jnp.minimum(cnt_e - j * bm, bm)
                return c
            lax.fori_loop(0, nb_e, fill, 0)

        start = start + nb_e
    dest_ref[...] = dest

    @pl.when(first)
    def _():
        def fill(j, c):
            be_ref[j] = n_exp - 1
            nv_ref[j] = 0
            return c
        lax.fori_loop(start, n_blocks, fill, 0)


def _moe_plan(cnt, tope, rank, bm, n_blocks):
    n = tope.shape[1]
    n_exp = cnt.shape[0]
    tl = min(2048, n)
    col = pl.BlockSpec((SUBLANES, tl), lambda i: (0, i))
    smem = lambda: pl.BlockSpec(memory_space=pltpu.SMEM)
    return pl.pallas_call(
        functools.partial(_plan_kernel, n_exp=n_exp, bm=bm, n_blocks=n_blocks),
        grid=(n // tl,),
        in_specs=[smem(), col, col],
        out_specs=[col, smem(), smem()],
        out_shape=[jax.ShapeDtypeStruct((SUBLANES, n), I32), jax.ShapeDtypeStruct((n_blocks,), I32),
                   jax.ShapeDtypeStruct((n_blocks,), I32)],
        compiler_params=_params("arbitrary"),
        name="moe_plan",
    )(cnt[:, 0], tope, rank)


def _tile_copy(src, src_row, dst, dst_row, sem):
    return pltpu.make_async_copy(src.at[pl.ds(src_row * SUBLANES, SUBLANES), :],
                                 dst.at[pl.ds(dst_row * SUBLANES, SUBLANES), :], sem)


def _dispatch_kernel(cnt_ref, dest_ref, x_ref, xs_ref, zero_ref, sem, zsem, *, tm, bm, n_exp, n_blocks):
    @pl.when(pl.program_id(0) == 0)
    def _():
        zero_ref[...] = jnp.zeros_like(zero_ref)

        def block_fill(j):
            return pltpu.make_async_copy(zero_ref, xs_ref.at[pl.ds(j * (bm * SUBLANES), bm * SUBLANES), :], zsem)

        def fills(act):
            start = jnp.int32(0)
            for e in range(n_exp):
                nb_e = (cnt_ref[e] + (bm - 1)) // bm
                start = start + nb_e

                @pl.when(nb_e > 0)
                def _(last=start - 1):
                    act(block_fill(last))

            def tail(j, c):
                act(block_fill(j))
                return c
            lax.fori_loop(start, n_blocks, tail, 0)

        fills(lambda cp: cp.start())
        fills(lambda cp: cp.wait())

    def issue(t, c):
        for k in range(TOP_K):
            _tile_copy(x_ref, t, xs_ref, dest_ref[k, t], sem).start(priority=k % 2)
        return c

    lax.fori_loop(0, tm, issue, 0, unroll=2)

    def drain(t, c):
        for k in range(TOP_K):
            _tile_copy(x_ref, t, xs_ref, dest_ref[k, t], sem).wait()
        return c

    lax.fori_loop(0, tm, drain, 0)


def _moe_dispatch(h2t, dest, cnt, bm, n_blocks):
    n = h2t.shape[0] // SUBLANES
    tm = min(MOVE_TILE, n)
    n_exp = cnt.shape[0]
    return pl.pallas_call(
        functools.partial(_dispatch_kernel, tm=tm, bm=bm, n_exp=n_exp, n_blocks=n_blocks),
        grid=(n // tm,),
        in_specs=[pl.BlockSpec(memory_space=pltpu.SMEM),
                  pl.BlockSpec((SUBLANES, tm), lambda i: (0, i), memory_space=pltpu.SMEM),
                  pl.BlockSpec((tm * SUBLANES, LANES), lambda i: (i, 0))],
        out_specs=pl.BlockSpec(memory_space=pl.ANY),
        out_shape=jax.ShapeDtypeStruct((n_blocks * bm * SUBLANES, LANES), F32),
        scratch_shapes=[pltpu.VMEM((bm * SUBLANES, LANES), F32), pltpu.SemaphoreType.DMA, pltpu.SemaphoreType.DMA],
        compiler_params=_params("arbitrary"),
        name="moe_dispatch",
    )(cnt[:, 0], dest, h2t)


def _expert_kernel(be_ref, nv_ref, x_ref, wg_ref, bg_ref, wu_ref, bu_ref, wd_ref, bd_ref, y_ref, *, bm):
    del be_ref
    n_valid = nv_ref[pl.program_id(0)]

    @pl.when(n_valid > 0)
    def _():
        x = _load_row_tiles(x_ref, bm).astype(BF16)
        g = jnp.minimum(_mm(x, wg_ref[0]) + bg_ref[0], SWIGLU_LIMIT)
        u = jnp.clip(_mm(x, wu_ref[0]) + bu_ref[0], -SWIGLU_LIMIT, SWIGLU_LIMIT)
        a = g * _sigmoid(SWIGLU_ALPHA * g) * (u + 1.0)
        _store_row_tiles(y_ref, _mm(a, wd_ref[0]) + bd_ref[0])

    @pl.when(n_valid <= 0)
    def _():
        y_ref[...] = jnp.zeros_like(y_ref)


def _moe_experts(xs, block_e, n_valid, w_gate, b_gate, w_up, b_up, w_down, b_down, bm):
    n_exp, d, d_ff = w_gate.shape
    assert d == SUBLANES * LANES, "a model row must be exactly one (SUBLANES, LANES) tile"
    n_blocks = xs.shape[0] // (bm * SUBLANES)
    wspec = lambda r, c: pl.BlockSpec((1, r, c), lambda i, be, nv: (be[i], 0, 0))
    rows = pl.BlockSpec((bm * SUBLANES, LANES), lambda i, be, nv: (i, 0))
    return pl.pallas_call(
        functools.partial(_expert_kernel, bm=bm),
        grid_spec=pltpu.PrefetchScalarGridSpec(
            num_scalar_prefetch=2,
            grid=(n_blocks,),
            in_specs=[rows, wspec(d, d_ff), wspec(1, d_ff), wspec(d, d_ff), wspec(1, d_ff), wspec(d_ff, d), wspec(1, d)],
            out_specs=rows),
        out_shape=jax.ShapeDtypeStruct(xs.shape, F32),
        compiler_params=_params("arbitrary"),
        name="moe_experts",
    )(block_e, n_valid, xs, w_gate.astype(BF16), b_gate.reshape(n_exp, 1, d_ff), w_up.astype(BF16),
      b_up.reshape(n_exp, 1, d_ff), w_down.astype(BF16), b_down.reshape(n_exp, 1, d))


def _combine_kernel(dest_ref, dnext_ref, gate_ref, x_ref, nw_ref, ys_ref, o_ref, buf_a, buf_b, sem_a, sem_b, *, tm):
    i = pl.program_id(0)
    n = pl.num_programs(0)

    def gather(d_ref, buf, sem, act):
        def body(t, c):
            for k in range(TOP_K):
                act(_tile_copy(ys_ref, d_ref[k, t], buf.at[k], t, sem), k)
            return c
        return body

    def start(d_ref, buf, sem):
        lax.fori_loop(0, tm, gather(d_ref, buf, sem, lambda cp, k: cp.start(priority=k % 2)), 0, unroll=2)

    def step(cur_buf, cur_sem, nxt_buf, nxt_sem):
        @pl.when(i + 1 < n)
        def _():
            start(dnext_ref, nxt_buf, nxt_sem)

        lax.fori_loop(0, tm, gather(dest_ref, cur_buf, cur_sem, lambda cp, k: cp.wait()), 0)
        gt = gate_ref[...].T
        acc = x_ref[...]
        for k in range(TOP_K):
            acc = acc + gt[:, k:k + 1] * _load_row_tiles(cur_buf.at[k], tm)
        o_ref[...] = _rms(acc, nw_ref[...])

    @pl.when(i == 0)
    def _():
        start(dest_ref, buf_a, sem_a)

    @pl.when(i % 2 == 0)
    def _():
        step(buf_a, sem_a, buf_b, sem_b)

    @pl.when(i % 2 == 1)
    def _():
        step(buf_b, sem_b, buf_a, sem_a)


def _moe_combine(ys, dest, gates, x2, norm_final):
    n, d = x2.shape
    tm = min(MOVE_TILE, n)
    nt = n // tm
    col = lambda ms: pl.BlockSpec((SUBLANES, tm), lambda i: (0, i), memory_space=ms)
    nxt = pl.BlockSpec((SUBLANES, tm), lambda i: (0, jnp.minimum(i + 1, nt - 1)), memory_space=pltpu.SMEM)
    buf = pltpu.VMEM((TOP_K, tm * SUBLANES, LANES), F32)
    return pl.pallas_call(
        functools.partial(_combine_kernel, tm=tm),
        grid=(nt,),
        in_specs=[col(pltpu.SMEM), nxt, col(pltpu.VMEM), pl.BlockSpec((tm, d), lambda i: (i, 0)), _full((1, d)),
                  pl.BlockSpec(memory_space=pl.ANY)],
        out_specs=pl.BlockSpec((tm, d), lambda i: (i, 0)),
        out_shape=jax.ShapeDtypeStruct((n, d), F32),
        scratch_shapes=[buf, buf, pltpu.SemaphoreType.DMA, pltpu.SemaphoreType.DMA],
        compiler_params=_params("arbitrary"),
        name="moe_combine",
    )(dest, dest, gates, x2, norm_final.reshape(1, d), ys)


def _moe_and_final_norm(x2, h2t, tope, gates, rank, cnt, w_gate, b_gate, w_up, b_up, w_down, b_down, norm_final):
    n = x2.shape[0]
    n_exp = w_gate.shape[0]
    bm = MOE_BLOCK_ROWS
    n_blocks = -(-(n * TOP_K + n_exp * (bm - 1)) // bm)
    dest, block_e, n_valid = _moe_plan(cnt, tope, rank, bm, n_blocks)
    xs = _moe_dispatch(h2t, dest, cnt, bm, n_blocks)
    ys = _moe_experts(xs, block_e, n_valid, w_gate, b_gate, w_up, b_up, w_down, b_down, bm)
    return _moe_combine(ys, dest, gates, x2, norm_final)


def kernel(x_prompt, x_sample, mem_prompt, cache_k, cache_v, page_table, state_delta, state_conv, cache_mem_k, cache_mem_v, norm_mix, w_in, conv_w, a_log, dt_bias, dn_norm, w_out, norm_x, norm_mem, w_xq, w_xk, w_xv, w_xo, norm_ffn, w_router, b_router, w_gate, b_gate, w_up, b_up, w_down, b_down, norm_final):
    depth = w_in.shape[0]
    assert depth == 1, "single-layer stack"
    b, s, d = x_prompt.shape
    bd, t, _ = x_sample.shape
    n_mem = mem_prompt.shape[1]
    n_pool, page = cache_k.shape[1], cache_k.shape[2]
    past = page_table.shape[1] * page
    assert past % MOBA_BLOCK == 0, "past length must be whole MoBA blocks"
    l = 0
    moe_w = (w_gate[l], b_gate[l], w_up[l], b_up[l], w_down[l], b_down[l])

    xs = x_sample.reshape(bd * t, d)
    tile_s = min(ROW_TILE, bd * t)
    cos_s, sin_s = _rope_tables(past + jnp.arange(tile_s) % t)
    conv_s, z_s, gb_s, q_s, k4_s, v4_s = _in_proj(xs, norm_mix[l], w_in[l], a_log[l], dt_bias[l],
                                                  cos_s, sin_s, prompt=False)
    dn_s, s_s, cv_s = _deltanet_sample(conv_s, z_s, gb_s, state_conv[l], state_delta[l], conv_w[l], dn_norm[l])
    mb_s = _moba_sample(q_s, k4_s, v4_s, cache_k.reshape(n_pool, page * HEADS, DH),
                        cache_v.reshape(n_pool, page * HEADS, DH), page_table)
    x2s, h2s, tope_s, gates_s, rank_s, cnt_s = _post_mix(
        dn_s, mb_s, xs, cache_mem_k.reshape(bd, n_mem * HEADS, DH), cache_mem_v.reshape(bd, n_mem * HEADS, DH),
        w_out[l], norm_x[l], w_xq[l], w_xo[l], norm_ffn[l], w_router[l], b_router[l])
    y_sample = _moe_and_final_norm(x2s, h2s, tope_s, gates_s, rank_s, cnt_s, *moe_w, norm_final)

    xp = x_prompt.reshape(b * s, d)
    cos_p, sin_p = _rope_tables(jnp.arange(s))
    conv_in, z, gb, q, k, v, kb, vt, kmean = _in_proj(xp, norm_mix[l], w_in[l], a_log[l], dt_bias[l],
                                                      cos_p, sin_p, prompt=True)
    dn, s_p, cv_p = _deltanet_prompt(conv_in, z, gb, conv_w[l], dn_norm[l], b)
    mb = _moba_prompt(q, kb, vt, kmean, b)
    mk_p, mv_p = _mem_kv(mem_prompt.reshape(b * n_mem, d), norm_mem[l], w_xk[l], w_xv[l])
    x2, h2, tope, gates, rank, cnt = _post_mix(
        dn, mb, xp, mk_p.reshape(b, n_mem * HEADS, DH), mv_p.reshape(b, n_mem * HEADS, DH), w_out[l], norm_x[l],
        w_xq[l], w_xo[l], norm_ffn[l], w_router[l], b_router[l])
    y_prompt = _moe_and_final_norm(x2, h2, tope, gates, rank, cnt, *moe_w, norm_final)

    hd = (HEADS, DH)
    return (y_prompt.reshape(b, s, d), y_sample.reshape(bd, t, d),
            k.reshape(1, b, s, *hd), v.reshape(1, b, s, *hd),
            k4_s.reshape(1, bd, t, *hd), v4_s.reshape(1, bd, t, *hd),
            s_p[None], s_s[None], cv_p[None], cv_s[None],
            mk_p.reshape(1, b, n_mem, *hd), mv_p.reshape(1, b, n_mem, *hd))
```
